```python
import math
import jax
import jax.numpy as jnp
from jax import lax
import numpy as np

D_MODEL = 1024
BATCH = 8
SEQ = 2048
DEPTH = 4

MEM_LEN = 256
NORM_EPS = 1e-6
DOC_MEAN_LEN = 512
MASK_VALUE = -1e30
GATE_FLOOR = 1e-30

N_BRANCH = 4
BRANCH_W = D_MODEL // 4

MLA_HEADS = 4
MLA_Q_RANK = D_MODEL // 4
MLA_KV_RANK = D_MODEL // 8
MLA_NOPE = 64
MLA_ROPE = 32
MLA_V = BRANCH_W // MLA_HEADS
ROPE_THETA = 10000.0
Q_BLOCK = 128

HG_HEADS = 4
HG_DIM = BRANCH_W // HG_HEADS
HG_CHUNK = 16

ML_HEADS = 4
ML_DIM = BRANCH_W // ML_HEADS
ML_CONV = 4
ML_CHUNK = 64

S5_GROUP_CH = 16
S5_GROUPS = BRANCH_W // S5_GROUP_CH
S5_STATE = 64
S5_DT_MIN = 0.001
S5_DT_MAX = 0.1

XA_HEADS = 4
XA_DIM = D_MODEL // XA_HEADS

D_FF = ((8 * D_MODEL + 3 * 256 - 1) // (3 * 256)) * 256

SPLIT_SIZES = (MLA_Q_RANK, MLA_KV_RANK, MLA_ROPE,
               BRANCH_W, BRANCH_W, BRANCH_W, BRANCH_W,
               BRANCH_W, BRANCH_W, BRANCH_W, ML_HEADS, ML_HEADS,
               BRANCH_W,
               N_BRANCH * D_MODEL)
N_IN = sum(SPLIT_SIZES)

kernel_name = 'hybrid_mla_hgrn2_mlstm_s5_block'


def rms_norm(x, gain):
    xf = x.astype(jnp.float32)
    y = xf * lax.rsqrt(jnp.mean(xf * xf, axis=-1, keepdims=True) + NORM_EPS)
    return (y * gain.astype(jnp.float32)).astype(x.dtype)


def apply_rope(x, positions):
    half = x.shape[-1] // 2
    inv_freq = ROPE_THETA ** (-jnp.arange(half, dtype=jnp.float32) / half)
    ang = positions.astype(jnp.float32)[:, :, None, None] * inv_freq
    cos, sin = jnp.cos(ang), jnp.sin(ang)
    xf = x.astype(jnp.float32)
    x1, x2 = xf[..., :half], xf[..., half:]
    return jnp.concatenate([x1 * cos - x2 * sin, x2 * cos + x1 * sin], axis=-1).astype(x.dtype)


def causal_block_attention(q, k, v, scale):
    B, T, H, _ = q.shape
    nb = T // Q_BLOCK
    qb = q.reshape(B, nb, Q_BLOCK, H, -1).swapaxes(0, 1)
    kpos = jnp.arange(T)

    def one_block(args):
        qi, bi = args
        s = jnp.einsum('bqhd,bkhd->bhqk', qi, k).astype(jnp.float32) * scale
        qpos = bi * Q_BLOCK + jnp.arange(Q_BLOCK)
        s = jnp.where(kpos[None, :] <= qpos[:, None], s, MASK_VALUE)
        p = jax.nn.softmax(s, axis=-1).astype(v.dtype)
        return jnp.einsum('bhqk,bkhd->bqhd', p, v)

    out = lax.map(one_block, (qb, jnp.arange(nb)))
    return out.swapaxes(0, 1).reshape(B, T, H, -1)


def hgrn2_chunked(q, k, v, log_f):
    B, T, H, Dk = q.shape
    Dv = v.shape[-1]
    L = HG_CHUNK
    nc = T // L
    q, k, v, log_f = (a.reshape(B, nc, L, H, a.shape[-1]) for a in (q, k, v, log_f))
    b = jnp.cumsum(log_f, axis=2)
    b_end = b[:, :, -1]
    causal = jnp.tril(jnp.ones((L, L), dtype=bool))
    diff = b[:, :, :, None] - b[:, :, None]
    decay = jnp.exp(jnp.where(causal[None, None, :, :, None, None], diff, MASK_VALUE))
    scores = jnp.einsum('bcthd,bctshd,bcshd->bchts', q, decay, k)
    o_intra = jnp.einsum('bchts,bcshe->bcthe', scores, v)
    kv_end = jnp.einsum('bcshd,bcshe->bchde', k * jnp.exp(b_end[:, :, None] - b), v)

    def step(S, xs):
        dec, kv = xs
        return jnp.exp(dec)[..., None] * S + kv, S

    _, S0 = lax.scan(step, jnp.zeros((B, H, Dk, Dv), jnp.float32),
                     (b_end.swapaxes(0, 1), kv_end.swapaxes(0, 1)))
    S0 = S0.swapaxes(0, 1)
    o_inter = jnp.einsum('bcthd,bchde->bcthe', q * jnp.exp(b), S0)
    return (o_intra + o_inter).reshape(B, T, H, Dv)


def mlstm_chunked(q, k, v, i_log, f_log):
    B, T, H, Dk = q.shape
    Dv = v.shape[-1]
    L = ML_CHUNK
    nc = T // L
    q, k, v = (a.reshape(B, nc, L, H, a.shape[-1]) for a in (q, k, v))
    i_log = i_log.reshape(B, nc, L, H)
    b = jnp.cumsum(f_log.reshape(B, nc, L, H), axis=2)
    b_end = b[:, :, -1]
    g_end = b_end[:, :, None] - b + i_log

    def step(carry, xs):
        C, n, m = carry
        be, ge, kc, vc = xs
        m_new = jnp.maximum(be + m, ge.max(axis=1))
        keep = jnp.exp(be + m - m_new)
        w = jnp.exp(ge - m_new[:, None])
        C_new = keep[..., None, None] * C + jnp.einsum('blh,blhd,blhe->bhde', w, kc, vc)
        n_new = keep[..., None] * n + jnp.einsum('blh,blhd->bhd', w, kc)
        return (C_new, n_new, m_new), (C, n, m)

    init = (jnp.zeros((B, H, Dk, Dv), jnp.float32), jnp.zeros((B, H, Dk), jnp.float32),
            jnp.zeros((B, H), jnp.float32))
    _, (C0, n0, m0) = lax.scan(step, init, (b_end.swapaxes(0, 1), g_end.swapaxes(0, 1),
                                            k.swapaxes(0, 1), v.swapaxes(0, 1)))
    C0, n0, m0 = C0.swapaxes(0, 1), n0.swapaxes(0, 1), m0.swapaxes(0, 1)
    causal = jnp.tril(jnp.ones((L, L), dtype=bool))
    dmat = b[:, :, :, None, :] - b[:, :, None, :, :] + i_log[:, :, None, :, :]
    dmat = jnp.where(causal[None, None, :, :, None], dmat, MASK_VALUE)
    a = b + m0[:, :, None, :]
    m_t = jnp.maximum(a, dmat.max(axis=3))
    w_intra = jnp.exp(dmat - m_t[:, :, :, None, :])
    w_inter = jnp.exp(a - m_t)
    qk = jnp.einsum('bcthd,bcshd->bctsh', q, k) * w_intra
    num = (jnp.einsum('bctsh,bcshe->bcthe', qk, v)
           + w_inter[..., None] * jnp.einsum('bcthd,bchde->bcthe', q, C0))
    den = qk.sum(axis=3) + w_inter * jnp.einsum('bcthd,bchd->bcth', q, n0)
    h = num / jnp.maximum(jnp.abs(den), jnp.exp(-m_t))[..., None]
    return h.reshape(B, T, H, Dv)


def causal_depthwise_conv(x, w, b):
    y = lax.conv_general_dilated(x, w.astype(x.dtype)[:, None, :], window_strides=(1,),
                                 padding=((ML_CONV - 1, 0),),
                                 dimension_numbers=('NWC', 'WIO', 'NWC'),
                                 feature_group_count=x.shape[-1])
    return y + b.astype(x.dtype)


def _complex_linear_combine(e1, e2):
    ar1, ai1, br1, bi1 = e1
    ar2, ai2, br2, bi2 = e2
    return (ar2 * ar1 - ai2 * ai1, ar2 * ai1 + ai2 * ar1,
            ar2 * br1 - ai2 * bi1 + br2, ar2 * bi1 + ai2 * br1 + bi2)


def s5_ssm(u, a_re, a_im, log_dt, b_re, b_im, c_re, c_im, d):
    f32 = jnp.float32
    a_re, a_im = a_re.astype(f32), a_im.astype(f32)
    b_re, b_im, c_re, c_im = b_re.astype(f32), b_im.astype(f32), c_re.astype(f32), c_im.astype(f32)
    dt = jnp.exp(log_dt.astype(f32))[:, None]
    mag = jnp.exp(a_re * dt)
    ab_re, ab_im = mag * jnp.cos(a_im * dt), mag * jnp.sin(a_im * dt)
    inv_abs2 = 1.0 / (a_re * a_re + a_im * a_im)
    z_re = ((ab_re - 1.0) * a_re + ab_im * a_im) * inv_abs2
    z_im = (ab_im * a_re - (ab_re - 1.0) * a_im) * inv_abs2
    bb_re = z_re[..., None] * b_re - z_im[..., None] * b_im
    bb_im = z_re[..., None] * b_im + z_im[..., None] * b_re
    x_re = jnp.einsum('gph,btgh->btgp', bb_re, u)
    x_im = jnp.einsum('gph,btgh->btgp', bb_im, u)
    _, _, s_re, s_im = lax.associative_scan(
        _complex_linear_combine,
        (jnp.broadcast_to(ab_re, x_re.shape), jnp.broadcast_to(ab_im, x_im.shape), x_re, x_im),
        axis=1)
    return (jnp.einsum('ghp,btgp->btgh', c_re, s_re) - jnp.einsum('ghp,btgp->btgh', c_im, s_im)
            + d.astype(f32) * u)


def hybrid_token_mixer(h, positions, lower_bound, w_in, mla_q_norm, mla_w_uq, mla_kv_norm, mla_w_ukv,
                       hg_out_norm, ml_conv_w, ml_conv_b, ml_w_q, ml_w_k, ml_i_bias, ml_f_bias,
                       ml_out_norm, s5_a_re, s5_a_im, s5_log_dt, s5_b_re, s5_b_im, s5_c_re, s5_c_im,
                       s5_d, s5_w_glu, s5_b_glu, w_branch, w_out):
    B, T, _ = h.shape
    f32 = jnp.float32
    split_points = np.cumsum(SPLIT_SIZES)[:-1].tolist()
    (a_q, a_kv, a_kr, b_q, b_f, b_i, b_g, c_x, c_v, c_o, c_ig, c_fg, d_u,
     gate_logits) = jnp.split(h @ w_in, split_points, axis=-1)
    heads = lambda a, n: a.reshape(B, T, n, -1)

    q = (rms_norm(a_q, mla_q_norm) @ mla_w_uq).reshape(B, T, MLA_HEADS, MLA_NOPE + MLA_ROPE)
    q = jnp.concatenate([q[..., :MLA_NOPE], apply_rope(q[..., MLA_NOPE:], positions)], axis=-1)
    kv = (rms_norm(a_kv, mla_kv_norm) @ mla_w_ukv).reshape(B, T, MLA_HEADS, MLA_NOPE + MLA_V)
    k_rope = apply_rope(a_kr[:, :, None, :], positions)
    k = jnp.concatenate([kv[..., :MLA_NOPE],
                         jnp.broadcast_to(k_rope, (B, T, MLA_HEADS, MLA_ROPE))], axis=-1)
    out_a = causal_block_attention(q, k, kv[..., MLA_NOPE:],
                                   (MLA_NOPE + MLA_ROPE) ** -0.5).reshape(B, T, BRANCH_W)

    f_logit = b_f.astype(f32)
    lb = lower_bound.astype(f32)
    forget = lb + (1.0 - lb) * jax.nn.sigmoid(f_logit)
    log_f = jnp.log(jnp.maximum(forget, GATE_FLOOR))
    k_hg = (1.0 - lb) * jax.nn.sigmoid(-f_logit)
    o_hg = hgrn2_chunked(heads(jax.nn.silu(b_q.astype(f32)), HG_HEADS), heads(k_hg, HG_HEADS),
                         heads(b_i.astype(f32), HG_HEADS), heads(log_f, HG_HEADS))
    out_b = (rms_norm(o_hg, hg_out_norm.reshape(HG_HEADS, HG_DIM)).reshape(B, T, BRANCH_W)
             * jax.nn.silu(b_g.astype(f32)))

    xc = jax.nn.silu(causal_depthwise_conv(c_x, ml_conv_w, ml_conv_b)).reshape(B, T, ML_HEADS, ML_DIM)
    q_ml = jnp.einsum('bthd,hde->bthe', xc, ml_w_q).astype(f32)
    k_ml = jnp.einsum('bthd,hde->bthe', xc, ml_w_k).astype(f32) * ML_DIM ** -0.5
    i_log = c_ig.astype(f32) + ml_i_bias.astype(f32)
    f_log = jax.nn.log_sigmoid(c_fg.astype(f32) + ml_f_bias.astype(f32))
    h_ml = mlstm_chunked(q_ml, k_ml, heads(c_v.astype(f32), ML_HEADS), i_log, f_log)
    out_c = (jax.nn.sigmoid(c_o.astype(f32))
             * rms_norm(h_ml, ml_out_norm.reshape(ML_HEADS, ML_DIM)).reshape(B, T, BRANCH_W))

    u = d_u.astype(f32).reshape(B, T, S5_GROUPS, S5_GROUP_CH)
    y = jax.nn.gelu(s5_ssm(u, s5_a_re, s5_a_im, s5_log_dt, s5_b_re, s5_b_im, s5_c_re, s5_c_im,
                           s5_d).reshape(B, T, BRANCH_W))
    out_d = y * jax.nn.sigmoid(y @ s5_w_glu.astype(f32) + s5_b_glu.astype(f32))

    gates = jax.nn.sigmoid(gate_logits.astype(f32)).reshape(B, T, N_BRANCH, D_MODEL)
    branches = (out_a, out_b, out_c, out_d)
    merged = sum(gates[:, :, n] * (branches[n].astype(h.dtype) @ w_branch[n]) for n in range(N_BRANCH))
    return merged.astype(h.dtype) @ w_out


def memory_cross_attention(h, mem_n, wq, wk, wv, wo):
    B, T, _ = h.shape
    M = mem_n.shape[1]
    q = (h @ wq).reshape(B, T, XA_HEADS, XA_DIM)
    k = (mem_n @ wk).reshape(B, M, XA_HEADS, XA_DIM)
    v = (mem_n @ wv).reshape(B, M, XA_HEADS, XA_DIM)
    s = jnp.einsum('bthd,bmhd->bhtm', q, k).astype(jnp.float32) * XA_DIM ** -0.5
    p = jax.nn.softmax(s, axis=-1).astype(v.dtype)
    return jnp.einsum('bhtm,bmhd->bthd', p, v).reshape(B, T, D_MODEL) @ wo


def swiglu_ffn(h, w_in, w_out):
    a, b = jnp.split(h @ w_in, 2, axis=-1)
    return (jax.nn.silu(a) * b) @ w_out


def setup_inputs(seed: int = 0) -> dict:
    key = jax.random.key(seed)
    ks = iter(jax.random.split(key, 64))
    f32 = jnp.float32
    nrm = lambda shape, scale: scale * jax.random.normal(next(ks), shape, f32)
    gain = lambda shape: 1.0 + 0.1 * jax.random.normal(next(ks), shape, f32)
    L = DEPTH
    t = jnp.arange(SEQ, dtype=jnp.int32)
    starts = jax.random.bernoulli(next(ks), 1.0 / DOC_MEAN_LEN, (BATCH, SEQ)).at[:, 0].set(True)
    positions = (t - lax.cummax(jnp.where(starts, t, 0), axis=1)).astype(jnp.int32)
    return {
        'x': nrm((BATCH, SEQ, D_MODEL), 1.0),
        'mem': nrm((BATCH, MEM_LEN, D_MODEL), 1.0),
        'positions': positions,
        'norm_mix_pre': gain((L, D_MODEL)),
        'norm_mix_post': gain((L, D_MODEL)),
        'w_in': nrm((L, D_MODEL, N_IN), D_MODEL ** -0.5),
        'mla_q_norm': gain((L, MLA_Q_RANK)),
        'mla_w_uq': nrm((L, MLA_Q_RANK, MLA_HEADS * (MLA_NOPE + MLA_ROPE)), MLA_Q_RANK ** -0.5),
        'mla_kv_norm': gain((L, MLA_KV_RANK)),
        'mla_w_ukv': nrm((L, MLA_KV_RANK, MLA_HEADS * (MLA_NOPE + MLA_V)), MLA_KV_RANK ** -0.5),
        'hg_lb_logits': nrm((L, BRANCH_W), 0.1),
        'hg_out_norm': gain((L, BRANCH_W)),
        'ml_conv_w': nrm((L, ML_CONV, BRANCH_W), ML_CONV ** -0.5),
        'ml_conv_b': nrm((L, BRANCH_W), 0.01),
        'ml_w_q': nrm((L, ML_HEADS, ML_DIM, ML_DIM), ML_DIM ** -0.5),
        'ml_w_k': nrm((L, ML_HEADS, ML_DIM, ML_DIM), ML_DIM ** -0.5),
        'ml_i_bias': nrm((L, ML_HEADS), 0.1),
        'ml_f_bias': jnp.linspace(3.0, 6.0, ML_HEADS, dtype=f32)[None] + nrm((L, ML_HEADS), 0.1),
        'ml_out_norm': gain((L, BRANCH_W)),
        's5_a_re': -0.5 + nrm((L, S5_GROUPS, S5_STATE), 0.01),
        's5_a_im': jnp.pi * jnp.arange(S5_STATE, dtype=f32) + nrm((L, S5_GROUPS, S5_STATE), 0.01),
        's5_log_dt': jax.random.uniform(next(ks), (L, S5_GROUPS), f32,
                                        math.log(S5_DT_MIN), math.log(S5_DT_MAX)),
        's5_b_re': nrm((L, S5_GROUPS, S5_STATE, S5_GROUP_CH), (2 * S5_GROUP_CH) ** -0.5),
        's5_b_im': nrm((L, S5_GROUPS, S5_STATE, S5_GROUP_CH), (2 * S5_GROUP_CH) ** -0.5),
        's5_c_re': nrm((L, S5_GROUPS, S5_GROUP_CH, S5_STATE), S5_STATE ** -0.5),
        's5_c_im': nrm((L, S5_GROUPS, S5_GROUP_CH, S5_STATE), S5_STATE ** -0.5),
        's5_d': nrm((L, S5_GROUPS, S5_GROUP_CH), 1.0),
        's5_w_glu': nrm((L, BRANCH_W, BRANCH_W), BRANCH_W ** -0.5),
        's5_b_glu': nrm((L, BRANCH_W), 0.01),
        'w_branch': nrm((L, N_BRANCH, BRANCH_W, D_MODEL), BRANCH_W ** -0.5),
        'w_out': nrm((L, D_MODEL, D_MODEL), D_MODEL ** -0.5),
        'norm_xa_pre': gain((L, D_MODEL)),
        'norm_xa_post': gain((L, D_MODEL)),
        'norm_mem': gain((L, D_MODEL)),
        'xa_wq': nrm((L, D_MODEL, D_MODEL), D_MODEL ** -0.5),
        'xa_wk': nrm((L, D_MODEL, D_MODEL), D_MODEL ** -0.5),
        'xa_wv': nrm((L, D_MODEL, D_MODEL), D_MODEL ** -0.5),
        'xa_wo': nrm((L, D_MODEL, D_MODEL), D_MODEL ** -0.5),
        'norm_ffn_pre': gain((L, D_MODEL)),
        'norm_ffn_post': gain((L, D_MODEL)),
        'ffn_w_in': nrm((L, D_MODEL, 2 * D_FF), D_MODEL ** -0.5),
        'ffn_w_out': nrm((L, D_FF, D_MODEL), D_FF ** -0.5),
    }


def reference(x, mem, positions, norm_mix_pre, norm_mix_post, w_in, mla_q_norm, mla_w_uq,
              mla_kv_norm, mla_w_ukv, hg_lb_logits, hg_out_norm, ml_conv_w, ml_conv_b, ml_w_q,
              ml_w_k, ml_i_bias, ml_f_bias, ml_out_norm, s5_a_re, s5_a_im, s5_log_dt, s5_b_re,
              s5_b_im, s5_c_re, s5_c_im, s5_d, s5_w_glu, s5_b_glu, w_branch, w_out, norm_xa_pre,
              norm_xa_post, norm_mem, xa_wq, xa_wk, xa_wv, xa_wo, norm_ffn_pre, norm_ffn_post,
              ffn_w_in, ffn_w_out):
    lb_soft = jax.nn.softmax(hg_lb_logits.astype(jnp.float32), axis=0)
    lower_bounds = jnp.cumsum(lb_soft, axis=0) - lb_soft[0]
    for l in range(DEPTH):
        h = rms_norm(x, norm_mix_pre[l])
        mix = hybrid_token_mixer(h, positions, lower_bounds[l], w_in[l], mla_q_norm[l], mla_w_uq[l],
                                 mla_kv_norm[l], mla_w_ukv[l], hg_out_norm[l], ml_conv_w[l],
                                 ml_conv_b[l], ml_w_q[l], ml_w_k[l], ml_i_bias[l], ml_f_bias[l],
                                 ml_out_norm[l], s5_a_re[l], s5_a_im[l], s5_log_dt[l], s5_b_re[l],
                                 s5_b_im[l], s5_c_re[l], s5_c_im[l], s5_d[l], s5_w_glu[l],
                                 s5_b_glu[l], w_branch[l], w_out[l])
        x = x + rms_norm(mix, norm_mix_post[l])
        h = rms_norm(x, norm_xa_pre[l])
        xa = memory_cross_attention(h, rms_norm(mem, norm_mem[l]), xa_wq[l], xa_wk[l], xa_wv[l], xa_wo[l])
        x = x + rms_norm(xa, norm_xa_post[l])
        h = rms_norm(x, norm_ffn_pre[l])
        x = x + rms_norm(swiglu_ffn(h, ffn_w_in[l], ffn_w_out[l]), norm_ffn_post[l])
    return x
```

```python
import functools
import math

import numpy as np
import jax
import jax.numpy as jnp
from jax import lax
from jax.experimental import pallas as pl
from jax.experimental.pallas import tpu as pltpu

F32 = jnp.float32
BF16 = jnp.bfloat16

D_MODEL = 1024
NORM_EPS = 1e-6
MASK_VALUE = -1e30
GATE_FLOOR = 1e-30
BRANCH_W = 256
N_HEADS = 4
HEAD_W = 64
MLA_Q_RANK = 256
MLA_KV_RANK = 128
MLA_NOPE = 64
MLA_ROPE = 32
ROPE_THETA = 10000.0
ML_CONV = 4
S5_GROUPS = 16
S5_GROUP_CH = 16
S5_STATE = 64
S5_CHUNK = 16
XA_HEADS = 4
XA_DIM = 256
D_FF = 2816
FF_CHUNK = 256

LANES = 128
CHUNK = 64
SUB = 16
VMEM_LIMIT = 56 * 1024 * 1024

OFF_AQ, OFF_AKV, OFF_KR, OFF_KRROT, W_SEC_A = 0, 256, 384, 512, 640
W_SEC_B = 4 * BRANCH_W
W_SEC_C = 3 * BRANCH_W + LANES
W_SEC_D = BRANCH_W
W_MIX = W_SEC_A + W_SEC_B + W_SEC_C + W_SEC_D


def _nt_dot(a, b):
    return lax.dot_general(a, b, (((1,), (1,)), ((), ())), preferred_element_type=F32)


def _tn_dot(a, b):
    return lax.dot_general(a, b, (((0,), (0,)), ((), ())), preferred_element_type=F32)


def _dot(a, b):
    return jnp.dot(a, b, preferred_element_type=F32)


def _dot_split(a, b_bf16, terms=3):
    acc = None
    rem = a
    for _ in range(terms):
        piece = rem.astype(BF16)
        part = _dot(piece, b_bf16)
        acc = part if acc is None else acc + part
        rem = rem - piece.astype(F32)
    return acc


def _rms(x, gain):
    return x * lax.rsqrt(jnp.mean(x * x, axis=-1, keepdims=True) + NORM_EPS) * gain


def _sigmoid(x):
    return 1.0 / (1.0 + jnp.exp(-x))


def _silu(x):
    return x * _sigmoid(x)


def _log_sigmoid(x):
    return jnp.minimum(x, 0.0) - jnp.log(1.0 + jnp.exp(-jnp.abs(x)))


def _gelu_tanh(x):
    c = math.sqrt(2.0 / math.pi)
    return 0.5 * x * (1.0 + jnp.tanh(c * (x + 0.044715 * (x * x * x))))


def _head_block_ones(dtype):
    r = lax.broadcasted_iota(jnp.int32, (BRANCH_W, BRANCH_W), 0) // HEAD_W
    c = lax.broadcasted_iota(jnp.int32, (BRANCH_W, BRANCH_W), 1) // HEAD_W
    return (r == c).astype(dtype)


def _head_stack_mask(m, dtype):
    r = lax.broadcasted_iota(jnp.int32, (N_HEADS * m, BRANCH_W), 0) // m
    c = lax.broadcasted_iota(jnp.int32, (N_HEADS * m, BRANCH_W), 1) // HEAD_W
    return (r == c).astype(dtype)


def _head_norm(o, gain, ones_bd):
    msq = _dot_split(o * o, ones_bd, terms=2) * (1.0 / HEAD_W)
    return o * lax.rsqrt(msq + NORM_EPS) * gain


def _lower_tri(n, dtype):
    r = lax.broadcasted_iota(jnp.int32, (n, n), 0)
    c = lax.broadcasted_iota(jnp.int32, (n, n), 1)
    return (c <= r).astype(dtype)


def _front_kernel(x_ref, pos_ref, g_ref, wmix_ref, qn_ref, kvn_ref, wq_ref, wqr_ref, wk_ref, wv_ref,
                  freq_ref, q_out, k_out, v_out, pb_out, pc_out, pd_out):
    hn = _rms(x_ref[...], g_ref[...]).astype(BF16)
    pa = _dot(hn, wmix_ref[:, 0:W_SEC_A])
    pb_out[...] = _dot(hn, wmix_ref[:, W_SEC_A:W_SEC_A + W_SEC_B])
    pc_out[...] = _dot(hn, wmix_ref[:, W_SEC_A + W_SEC_B:W_SEC_A + W_SEC_B + W_SEC_C])
    pd_out[...] = _dot(hn, wmix_ref[:, W_SEC_A + W_SEC_B + W_SEC_C:W_MIX])

    ang = pos_ref[...].astype(F32) * freq_ref[...]
    cos, sin = jnp.cos(ang), jnp.sin(ang)
    scale = (MLA_NOPE + MLA_ROPE) ** -0.5

    aqn = _rms(pa[:, OFF_AQ:OFF_AQ + MLA_Q_RANK], qn_ref[...]).astype(BF16)
    q0 = _dot(aqn, wq_ref[...])
    qr = _dot(aqn, wqr_ref[...])
    akvn = _rms(pa[:, OFF_AKV:OFF_AKV + MLA_KV_RANK], kvn_ref[...]).astype(BF16)
    kn = _dot(akvn, wk_ref[...])
    v_out[...] = _dot(akvn, wv_ref[...]).astype(v_out.dtype)
    k_rope = pa[:, OFF_KR:OFF_KR + LANES] * cos + pa[:, OFF_KRROT:OFF_KRROT + LANES] * sin
    for h in range(N_HEADS):
        sl = slice(h * LANES, (h + 1) * LANES)
        q_out[:, sl] = ((q0[:, sl] * cos + qr[:, sl] * sin) * scale).astype(q_out.dtype)
        k_out[:, sl] = (kn[:, sl] + k_rope).astype(k_out.dtype)


def _const_spec(shape):
    zeros = (0,) * len(shape)
    return pl.BlockSpec(shape, lambda *_: zeros, pipeline_mode=pl.Buffered(1))


def _front(x2, pos2, lw, tm):
    n = x2.shape[0]
    row = lambda w: pl.BlockSpec((tm, w), lambda i: (i, 0))
    consts = [lw['g_mix_pre'], lw['w_mix'], lw['mla_q_norm'], lw['mla_kv_norm'], lw['wq'], lw['wqr'],
              lw['wk'], lw['wv'], lw['freq']]
    return pl.pallas_call(
        _front_kernel,
        grid=(n // tm,),
        in_specs=[row(D_MODEL), row(1)] + [_const_spec(c.shape) for c in consts],
        out_specs=[row(4 * LANES), row(4 * LANES), row(4 * LANES), row(W_SEC_B), row(W_SEC_C), row(W_SEC_D)],
        out_shape=[jax.ShapeDtypeStruct((n, 4 * LANES), BF16), jax.ShapeDtypeStruct((n, 4 * LANES), BF16),
                   jax.ShapeDtypeStruct((n, 4 * LANES), BF16), jax.ShapeDtypeStruct((n, W_SEC_B), F32),
                   jax.ShapeDtypeStruct((n, W_SEC_C), F32), jax.ShapeDtypeStruct((n, W_SEC_D), F32)],
        compiler_params=pltpu.CompilerParams(dimension_semantics=("parallel",), vmem_limit_bytes=VMEM_LIMIT),
        name="front",
    )(x2, pos2, *consts)


def _mla_kernel(q_ref, k_ref, v_ref, o_ref, *, blk):
    i = pl.program_id(1)
    row = lax.broadcasted_iota(jnp.int32, (blk, blk), 0)
    col = lax.broadcasted_iota(jnp.int32, (blk, blk), 1)
    outs = []
    for h in range(N_HEADS):
        sl = slice(h * LANES, (h + 1) * LANES)
        qh = q_ref[:, sl]

        def body(j, carry, sl=sl, qh=qh):
            m, l, acc = carry
            start = pl.multiple_of(j * blk, blk)
            kb = k_ref[pl.ds(start, blk), sl]
            vb = v_ref[pl.ds(start, blk), sl]
            s = _nt_dot(qh, kb)
            s = jnp.where(col + (j - i) * blk <= row, s, MASK_VALUE)
            m_new = jnp.maximum(m, jnp.max(s, axis=-1, keepdims=True))
            p = jnp.exp(s - m_new)
            alpha = jnp.exp(m - m_new)
            l = alpha * l + jnp.sum(p, axis=-1, keepdims=True)
            acc = alpha * acc + _dot(p.astype(BF16), vb)
            return m_new, l, acc

        init = (jnp.full((blk, 1), MASK_VALUE, F32), jnp.zeros((blk, 1), F32), jnp.zeros((blk, LANES), F32))
        _, l, acc = lax.fori_loop(0, i + 1, body, init)
        outs.append((acc / l)[:, :HEAD_W])
    o_ref[...] = jnp.concatenate(outs, axis=-1).astype(o_ref.dtype)


def _mla(q, k, v, batch, seq, blk):
    q3, k3, v3 = (a.reshape(batch, seq, 4 * LANES) for a in (q, k, v))
    out = pl.pallas_call(
        functools.partial(_mla_kernel, blk=blk),
        grid=(batch, seq // blk),
        in_specs=[pl.BlockSpec((None, blk, 4 * LANES), lambda b, i: (b, i, 0)),
                  pl.BlockSpec((None, seq, 4 * LANES), lambda b, i: (b, 0, 0)),
                  pl.BlockSpec((None, seq, 4 * LANES), lambda b, i: (b, 0, 0))],
        out_specs=pl.BlockSpec((None, blk, BRANCH_W), lambda b, i: (b, i, 0)),
        out_shape=jax.ShapeDtypeStruct((batch, seq, BRANCH_W), BF16),
        compiler_params=pltpu.CompilerParams(dimension_semantics=("parallel", "arbitrary"),
                                             vmem_limit_bytes=VMEM_LIMIT),
        name="mla",
    )(q3, k3, v3)
    return out.reshape(batch * seq, BRANCH_W)


def _hgrn_kernel(pb_ref, lbl_ref, gain_ref, o_ref, st_ref, *, layer, n_chunks):
    @pl.when(pl.program_id(1) == 0)
    def _():
        st_ref[...] = jnp.zeros_like(st_ref)

    lg = lbl_ref[...]
    e = jnp.exp(lg - jnp.max(lg, axis=0, keepdims=True))
    sm = e / jnp.sum(e, axis=0, keepdims=True)
    lb = jnp.zeros((1, BRANCH_W), F32)
    for r in range(1, layer + 1):
        lb = lb + sm[r:r + 1, :]

    ones_bd = _head_block_ones(BF16)
    ones_bd_f32 = _head_block_ones(F32)
    tri = _lower_tri(CHUNK, BF16)
    row_in_sub = lax.broadcasted_iota(jnp.int32, (CHUNK, BRANCH_W), 0) % SUB
    gain = gain_ref[...]

    def chunk_body(c, carry):
        start = pl.multiple_of(c * CHUNK, CHUNK)
        rows = pl.ds(start, CHUNK)
        q = _silu(pb_ref[rows, 0:BRANCH_W])
        f_logit = pb_ref[rows, BRANCH_W:2 * BRANCH_W]
        v = pb_ref[rows, 2 * BRANCH_W:3 * BRANCH_W]
        forget = lb + (1.0 - lb) * _sigmoid(f_logit)
        log_f = jnp.log(jnp.maximum(forget, GATE_FLOOR))
        k = (1.0 - lb) * _sigmoid(-f_logit)
        b = _cumsum_rows(log_f, tri)
        b_end = b[CHUNK - 1:CHUNK, :]

        o = jnp.zeros((CHUNK, BRANCH_W), F32)
        for lag in range(SUB):
            if lag == 0:
                k_s, b_s, v_s = k, b, v
            else:
                k_s = pltpu.roll(k, lag, 0)
                b_s = pltpu.roll(b, lag, 0)
                v_s = pltpu.roll(v, lag, 0)
            w = jnp.exp(jnp.where(row_in_sub >= lag, b - b_s, MASK_VALUE))
            score = _dot((q * k_s * w).astype(BF16), ones_bd)
            o = o + score * v_s

        m = SUB
        while m < CHUNK:
            mask = _head_stack_mask(m, F32)
            parts = []
            for lo in range(0, CHUNK, 2 * m):
                left = slice(lo, lo + m)
                right = slice(lo + m, lo + 2 * m)
                ref = b[lo + m - 1:lo + m, :]
                kt = k[left] * jnp.exp(ref - b[left])
                qt = q[right] * jnp.exp(b[right] - ref)
                k_hat = (jnp.concatenate([kt] * N_HEADS, axis=0) * mask).astype(BF16)
                v_hat = (jnp.concatenate([v[left]] * N_HEADS, axis=0) * mask).astype(BF16)
                a = _nt_dot(qt.astype(BF16), k_hat)
                parts.append(jnp.zeros((m, BRANCH_W), F32))
                parts.append(_dot(a.astype(BF16), v_hat))
            o = o + jnp.concatenate(parts, axis=0)
            m *= 2

        st = st_ref[...]
        o = o + _nt_dot((q * jnp.exp(b)).astype(BF16), st.astype(BF16))
        k_end = k * jnp.exp(b_end - b)
        st_ref[...] = (st * jnp.exp(b_end) + _tn_dot(v.astype(BF16), k_end.astype(BF16))) * ones_bd_f32

        gate = _silu(pb_ref[rows, 3 * BRANCH_W:4 * BRANCH_W])
        o_ref[rows, :] = (_head_norm(o, gain, ones_bd) * gate).astype(o_ref.dtype)
        return carry

    lax.fori_loop(0, n_chunks, chunk_body, 0)


def _cumsum_rows(x, tri):
    acc = None
    rem = x
    for _ in range(3):
        piece = rem.astype(BF16)
        part = _dot(tri, piece)
        acc = part if acc is None else acc + part
        rem = rem - piece.astype(F32)
    return acc


def _hgrn(pb, lb_logits, gain, layer, batch, seq, tb):
    pb3 = pb.reshape(batch, seq, W_SEC_B)
    out = pl.pallas_call(
        functools.partial(_hgrn_kernel, layer=layer, n_chunks=tb // CHUNK),
        grid=(batch, seq // tb),
        in_specs=[pl.BlockSpec((None, tb, W_SEC_B), lambda b, i: (b, i, 0)),
                  _const_spec(lb_logits.shape), _const_spec(gain.shape)],
        out_specs=pl.BlockSpec((None, tb, BRANCH_W), lambda b, i: (b, i, 0)),
        out_shape=jax.ShapeDtypeStruct((batch, seq, BRANCH_W), BF16),
        scratch_shapes=[pltpu.VMEM((BRANCH_W, BRANCH_W), F32)],
        compiler_params=pltpu.CompilerParams(dimension_semantics=("parallel", "arbitrary"),
                                             vmem_limit_bytes=VMEM_LIMIT),
        name="hgrn",
    )(pb3, lb_logits, gain)
    return out.reshape(batch * seq, BRANCH_W)


def _mlstm_kernel(pc_ref, cw_ref, cb_ref, wq_ref, wk_ref, ib_ref, fb_ref, gain_ref, o_ref,
                  xext_ref, q_ref, k_ref, c_ref, n_ref, m_ref, *, tb):
    pad = 8

    @pl.when(pl.program_id(1) == 0)
    def _():
        xext_ref[0:pad, :] = jnp.zeros((pad, BRANCH_W), F32)
        c_ref[...] = jnp.zeros_like(c_ref)
        n_ref[...] = jnp.zeros_like(n_ref)
        m_ref[...] = jnp.zeros_like(m_ref)

    xext_ref[pad:pad + tb, :] = pc_ref[:, 0:BRANCH_W]
    conv = jnp.zeros((tb, BRANCH_W), F32) + cb_ref[...]
    for j in range(ML_CONV):
        conv = conv + xext_ref[pad - (ML_CONV - 1) + j:pad - (ML_CONV - 1) + j + tb, :] * cw_ref[j:j + 1, :]
    xext_ref[0:pad, :] = xext_ref[tb:tb + pad, :]
    xc = _silu(conv).astype(BF16)
    q_ref[...] = _dot(xc, wq_ref[...])
    k_ref[...] = _dot(xc, wk_ref[...]) * (HEAD_W ** -0.5)

    ones_bd = _head_block_ones(BF16)
    ones_bd_f32 = _head_block_ones(F32)
    tri = _lower_tri(CHUNK, BF16)
    stack_mask = _head_stack_mask(CHUNK, F32)
    lane = lax.broadcasted_iota(jnp.int32, (CHUNK, BRANCH_W), 1)
    rowi = lax.broadcasted_iota(jnp.int32, (CHUNK, BRANCH_W), 0)
    diag_sel = (lane % HEAD_W == rowi).astype(F32)
    causal = (lane % HEAD_W) <= rowi
    gr = lax.broadcasted_iota(jnp.int32, (LANES, BRANCH_W), 0)
    gc = lax.broadcasted_iota(jnp.int32, (LANES, BRANCH_W), 1) // HEAD_W
    spread_i = (gr == gc).astype(BF16)
    spread_f = (gr == gc + N_HEADS).astype(BF16)
    gain = gain_ref[...]

    def head_max(a):
        out = jnp.zeros_like(a)
        for h in range(N_HEADS):
            mx = jnp.max(a[:, h * HEAD_W:(h + 1) * HEAD_W], axis=-1, keepdims=True)
            out = jnp.where(lane // HEAD_W == h, mx, out)
        return out

    def chunk_body(c, carry):
        rows = pl.ds(pl.multiple_of(c * CHUNK, CHUNK), CHUNK)
        q = q_ref[rows, :]
        k = k_ref[rows, :]
        v = pc_ref[rows, BRANCH_W:2 * BRANCH_W]
        gates = pc_ref[rows, 3 * BRANCH_W:3 * BRANCH_W + LANES]
        i_log = _dot_split(gates, spread_i) + ib_ref[...]
        f_log = _log_sigmoid(_dot_split(gates, spread_f) + fb_ref[...])
        b = _cumsum_rows(f_log, tri)
        b_end = b[CHUNK - 1:CHUNK, :]
        b_row = jnp.sum(b * diag_sel, axis=0, keepdims=True)
        i_row = jnp.sum(i_log * diag_sel, axis=0, keepdims=True)
        c0, n0, m0 = c_ref[...], n_ref[...], m_ref[...]

        dmat = jnp.where(causal, b - b_row + i_row, MASK_VALUE)
        a = b + m0
        m_t = jnp.maximum(a, head_max(dmat))
        w_intra = jnp.exp(dmat - m_t)
        w_inter = jnp.exp(a - m_t)
        qb = q.astype(BF16)
        k_hat = (jnp.concatenate([k] * N_HEADS, axis=0) * stack_mask).astype(BF16)
        v_hat = (jnp.concatenate([v] * N_HEADS, axis=0) * stack_mask).astype(BF16)
        qk = (_nt_dot(qb, k_hat) * w_intra).astype(BF16)
        num = _dot(qk, v_hat) + w_inter * _dot(qb, c0.astype(BF16))
        den = _dot(qk, ones_bd) + w_inter * _dot((q * n0).astype(BF16), ones_bd)
        h_out = num / jnp.maximum(jnp.abs(den), jnp.exp(-m_t))

        g_end = b_end - b + i_log
        m_new = jnp.maximum(b_end + m0, jnp.max(g_end, axis=0, keepdims=True))
        keep = jnp.exp(b_end + m0 - m_new)
        kw = k * jnp.exp(g_end - m_new)
        c_ref[...] = (keep * c0 + _tn_dot(kw.astype(BF16), v.astype(BF16))) * ones_bd_f32
        n_ref[...] = keep * n0 + jnp.sum(kw, axis=0, keepdims=True)
        m_ref[...] = m_new

        out_gate = _sigmoid(pc_ref[rows, 2 * BRANCH_W:3 * BRANCH_W])
        o_ref[rows, :] = (out_gate * _head_norm(h_out, gain, ones_bd)).astype(o_ref.dtype)
        return carry

    lax.fori_loop(0, tb // CHUNK, chunk_body, 0)


def _mlstm(pc, lw, batch, seq, tb):
    pc3 = pc.reshape(batch, seq, W_SEC_C)
    consts = [lw['ml_conv_w'], lw['ml_conv_b'], lw['ml_wq_bd'], lw['ml_wk_bd'], lw['ml_i_bias'],
              lw['ml_f_bias'], lw['ml_out_norm']]
    out = pl.pallas_call(
        functools.partial(_mlstm_kernel, tb=tb),
        grid=(batch, seq // tb),
        in_specs=[pl.BlockSpec((None, tb, W_SEC_C), lambda b, i: (b, i, 0))]
                 + [_const_spec(c.shape) for c in consts],
        out_specs=pl.BlockSpec((None, tb, BRANCH_W), lambda b, i: (b, i, 0)),
        out_shape=jax.ShapeDtypeStruct((batch, seq, BRANCH_W), BF16),
        scratch_shapes=[pltpu.VMEM((tb + 8, BRANCH_W), F32), pltpu.VMEM((tb, BRANCH_W), F32),
                        pltpu.VMEM((tb, BRANCH_W), F32), pltpu.VMEM((BRANCH_W, BRANCH_W), F32),
                        pltpu.VMEM((1, BRANCH_W), F32), pltpu.VMEM((1, BRANCH_W), F32)],
        compiler_params=pltpu.CompilerParams(dimension_semantics=("parallel", "arbitrary"),
                                             vmem_limit_bytes=VMEM_LIMIT),
        name="mlstm",
    )(pc3, *consts)
    return out.reshape(batch * seq, BRANCH_W)


def _s5_kernel(u_ref, tz_ref, min_re_ref, min_im_ref, mout_re_ref, mout_im_ref, pw_re_ref, pw_im_ref,
               d_ref, y_ref, *, rows_per_seq):
    n_rows = u_ref.shape[1]
    u = [u_ref[s] for s in range(2)]
    ub = [a.astype(BF16) for a in u]
    s_re = _dot(ub[0], min_re_ref[0]) + _dot(ub[1], min_re_ref[1])
    s_im = _dot(ub[0], min_im_ref[0]) + _dot(ub[1], min_im_ref[1])
    row = lax.broadcasted_iota(jnp.int32, (n_rows, LANES), 0) % rows_per_seq
    step, k = 1, 0
    while step < rows_per_seq:
        keep = row >= step
        p_re = jnp.where(keep, pltpu.roll(s_re, step, 0), 0.0)
        p_im = jnp.where(keep, pltpu.roll(s_im, step, 0), 0.0)
        a_re = pw_re_ref[k:k + 1, :]
        a_im = pw_im_ref[k:k + 1, :]
        s_re, s_im = s_re + a_re * p_re - a_im * p_im, s_im + a_re * p_im + a_im * p_re
        step, k = step * 2, k + 1
    first = row >= 1
    s0_re = jnp.where(first, pltpu.roll(s_re, 1, 0), 0.0).astype(BF16)
    s0_im = jnp.where(first, pltpu.roll(s_im, 1, 0), 0.0).astype(BF16)
    for s in range(2):
        y = _dot(ub[s], tz_ref[s]) + _dot(s0_re, mout_re_ref[s]) + _dot(s0_im, mout_im_ref[s])
        y_ref[s] = y + u[s] * d_ref[s]


def _s5(pd, lw, batch, seq):
    n = batch * seq
    rows = n // S5_CHUNK
    width = S5_CHUNK * S5_GROUP_CH
    ug = pd.reshape(rows, S5_CHUNK, S5_GROUPS, S5_GROUP_CH).transpose(2, 0, 1, 3).reshape(S5_GROUPS, rows, width)
    pair = lambda *tail: pl.BlockSpec((2,) + tail, lambda g: (g,) + (0,) * len(tail))
    y = pl.pallas_call(
        functools.partial(_s5_kernel, rows_per_seq=seq // S5_CHUNK),
        grid=(S5_GROUPS // 2,),
        in_specs=[pair(rows, width), pair(width, width), pair(width, LANES), pair(width, LANES),
                  pair(LANES, width), pair(LANES, width),
                  pl.BlockSpec((None, 8, LANES), lambda g: (g, 0, 0)),
                  pl.BlockSpec((None, 8, LANES), lambda g: (g, 0, 0)),
                  pair(1, width)],
        out_specs=pair(rows, width),
        out_shape=jax.ShapeDtypeStruct((S5_GROUPS, rows, width), F32),
        compiler_params=pltpu.CompilerParams(dimension_semantics=("parallel",), vmem_limit_bytes=VMEM_LIMIT),
        name="s5",
    )(ug, lw['s5_tz'], lw['s5_min_re'], lw['s5_min_im'], lw['s5_mout_re'], lw['s5_mout_im'],
      lw['s5_pw_re'], lw['s5_pw_im'], lw['s5_d'])
    return y.reshape(S5_GROUPS, rows, S5_CHUNK, S5_GROUP_CH).transpose(1, 2, 0, 3).reshape(n, BRANCH_W)


def _merge_kernel(x_ref, a_ref, b_ref, c_ref, y_ref, gpre_ref, wgate_ref, wbr_ref, wout_ref, wglu_ref,
                  bglu_ref, gpost_ref, o_ref):
    x = x_ref[...]
    hn = _rms(x, gpre_ref[...]).astype(BF16)
    y = _gelu_tanh(y_ref[...])
    out_d = (y * _sigmoid(_dot(y.astype(BF16), wglu_ref[...]) + bglu_ref[...])).astype(BF16)
    branches = (a_ref[...], b_ref[...], c_ref[...], out_d)
    merged = None
    for n in range(4):
        gate = _sigmoid(_dot(hn, wgate_ref[:, n * D_MODEL:(n + 1) * D_MODEL]))
        term = gate * _dot(branches[n], wbr_ref[n])
        merged = term if merged is None else merged + term
    mix = _dot(merged.astype(BF16), wout_ref[...])
    o_ref[...] = x + _rms(mix, gpost_ref[...])


def _merge(x2, out_a, out_b, out_c, y_s5, lw, tm):
    n = x2.shape[0]
    row = lambda w: pl.BlockSpec((tm, w), lambda i: (i, 0))
    consts = [lw['g_mix_pre'], lw['w_gate'], lw['w_branch'], lw['w_out'], lw['s5_w_glu'], lw['s5_b_glu'],
              lw['g_mix_post']]
    return pl.pallas_call(
        _merge_kernel,
        grid=(n // tm,),
        in_specs=[row(D_MODEL), row(BRANCH_W), row(BRANCH_W), row(BRANCH_W), row(BRANCH_W)]
                 + [_const_spec(c.shape) for c in consts],
        out_specs=row(D_MODEL),
        out_shape=jax.ShapeDtypeStruct((n, D_MODEL), F32),
        compiler_params=pltpu.CompilerParams(dimension_semantics=("parallel",), vmem_limit_bytes=VMEM_LIMIT),
        name="merge",
    )(x2, out_a, out_b, out_c, y_s5, *consts)


def _memkv_kernel(mem_ref, g_ref, wk_ref, wv_ref, k_out, v_out):
    mn = _rms(mem_ref[...], g_ref[...]).astype(BF16)
    k_out[...] = _dot(mn, wk_ref[...]).astype(k_out.dtype)
    v_out[...] = _dot(mn, wv_ref[...]).astype(v_out.dtype)


def _memkv(mem, lw):
    batch, mlen, _ = mem.shape
    consts = [lw['g_mem'], lw['xa_wk'], lw['xa_wv']]
    blk = pl.BlockSpec((None, mlen, D_MODEL), lambda b: (b, 0, 0))
    return pl.pallas_call(
        _memkv_kernel,
        grid=(batch,),
        in_specs=[blk] + [_const_spec(c.shape) for c in consts],
        out_specs=[blk, blk],
        out_shape=[jax.ShapeDtypeStruct((batch, mlen, D_MODEL), BF16)] * 2,
        compiler_params=pltpu.CompilerParams(dimension_semantics=("parallel",), vmem_limit_bytes=VMEM_LIMIT),
        name="memkv",
    )(mem, *consts)


def _xattn_kernel(x_ref, k_ref, v_ref, gpre_ref, wq_ref, wo_ref, gpost_ref, o_ref):
    x = x_ref[...]
    hn = _rms(x, gpre_ref[...]).astype(BF16)
    q = (_dot(hn, wq_ref[...]) * (XA_DIM ** -0.5)).astype(BF16)
    outs = []
    for h in range(XA_HEADS):
        sl = slice(h * XA_DIM, (h + 1) * XA_DIM)
        s = _nt_dot(q[:, sl], k_ref[:, sl])
        p = jnp.exp(s - jnp.max(s, axis=-1, keepdims=True))
        p = p / jnp.sum(p, axis=-1, keepdims=True)
        outs.append(_dot(p.astype(BF16), v_ref[:, sl]).astype(BF16))
    xa = _dot(jnp.concatenate(outs, axis=-1), wo_ref[...])
    o_ref[...] = x + _rms(xa, gpost_ref[...])


def _xattn(x2, mem_k, mem_v, lw, batch, seq, tm):
    mlen = mem_k.shape[1]
    x3 = x2.reshape(batch, seq, D_MODEL)
    consts = [lw['g_xa_pre'], lw['xa_wq'], lw['xa_wo'], lw['g_xa_post']]
    row = pl.BlockSpec((None, tm, D_MODEL), lambda b, i: (b, i, 0))
    kv = pl.BlockSpec((None, mlen, D_MODEL), lambda b, i: (b, 0, 0))
    out = pl.pallas_call(
        _xattn_kernel,
        grid=(batch, seq // tm),
        in_specs=[row, kv, kv] + [_const_spec(c.shape) for c in consts],
        out_specs=row,
        out_shape=jax.ShapeDtypeStruct((batch, seq, D_MODEL), F32),
        compiler_params=pltpu.CompilerParams(dimension_semantics=("parallel", "parallel"),
                                             vmem_limit_bytes=VMEM_LIMIT),
        name="xattn",
    )(x3, mem_k, mem_v, *consts)
    return out.reshape(batch * seq, D_MODEL)


def _ffn_kernel(x_ref, gpre_ref, wa_ref, wb_ref, wo_ref, gpost_ref, o_ref):
    x = x_ref[...]
    hn = _rms(x, gpre_ref[...]).astype(BF16)
    acc = None
    for j in range(D_FF // FF_CHUNK):
        sl = slice(j * FF_CHUNK, (j + 1) * FF_CHUNK)
        act = (_silu(_dot(hn, wa_ref[:, sl])) * _dot(hn, wb_ref[:, sl])).astype(BF16)
        part = _dot(act, wo_ref[sl, :])
        acc = part if acc is None else acc + part
    o_ref[...] = x + _rms(acc, gpost_ref[...])


def _ffn(x2, lw, tm):
    n = x2.shape[0]
    row = pl.BlockSpec((tm, D_MODEL), lambda i: (i, 0))
    consts = [lw['g_ffn_pre'], lw['ffn_wa'], lw['ffn_wb'], lw['ffn_wo'], lw['g_ffn_post']]
    return pl.pallas_call(
        _ffn_kernel,
        grid=(n // tm,),
        in_specs=[row] + [_const_spec(c.shape) for c in consts],
        out_specs=row,
        out_shape=jax.ShapeDtypeStruct((n, D_MODEL), F32),
        compiler_params=pltpu.CompilerParams(dimension_semantics=("parallel",), vmem_limit_bytes=VMEM_LIMIT),
        name="ffn",
    )(x2, *consts)


def _rot_half(w):
    half = MLA_ROPE // 2
    return jnp.concatenate([-w[..., half:], w[..., :half]], axis=-1)


def _block_diag(w):
    h, d, e = w.shape
    eye = jnp.eye(h, dtype=w.dtype)
    return (eye[:, None, :, None] * w[:, :, None, :]).reshape(h * d, h * e)


def _s5_tables(a_re, a_im, log_dt, b_re, b_im, c_re, c_im, d, rows_per_seq):
    hi = lax.Precision.HIGHEST
    g, p, hch = b_re.shape
    L = S5_CHUNK
    dt = jnp.exp(log_dt)[:, None]
    lam_re, lam_im = a_re * dt, a_im * dt
    mag = jnp.exp(lam_re)
    ab_re, ab_im = mag * jnp.cos(lam_im), mag * jnp.sin(lam_im)
    inv_abs2 = 1.0 / (a_re * a_re + a_im * a_im)
    z_re = ((ab_re - 1.0) * a_re + ab_im * a_im) * inv_abs2
    z_im = (ab_im * a_re - (ab_re - 1.0) * a_im) * inv_abs2
    bb_re = z_re[..., None] * b_re - z_im[..., None] * b_im
    bb_im = z_re[..., None] * b_im + z_im[..., None] * b_re

    def power(k):
        k = jnp.asarray(k, F32)[..., None, None]
        m = jnp.exp(k * lam_re)
        return m * jnp.cos(k * lam_im), m * jnp.sin(k * lam_im)

    pw_re, pw_im = power(jnp.arange(L + 1))
    cp_re = c_re[None] * pw_re[:, :, None, :] - c_im[None] * pw_im[:, :, None, :]
    cp_im = c_re[None] * pw_im[:, :, None, :] + c_im[None] * pw_re[:, :, None, :]
    kern = (jnp.einsum('kgap,gph->gkah', cp_re, bb_re, precision=hi)
            - jnp.einsum('kgap,gph->gkah', cp_im, bb_im, precision=hi))
    lag = jnp.arange(L)[None, :] - jnp.arange(L)[:, None]
    tz = jnp.where((lag >= 0)[None, :, :, None, None], kern[:, jnp.clip(lag, 0, L)], 0.0)
    tz = tz.transpose(0, 1, 4, 2, 3).reshape(g, L * hch, L * hch)

    rp_re, rp_im = pw_re[L - 1 - jnp.arange(L)], pw_im[L - 1 - jnp.arange(L)]
    min_re = rp_re[:, :, :, None] * bb_re[None] - rp_im[:, :, :, None] * bb_im[None]
    min_im = rp_re[:, :, :, None] * bb_im[None] + rp_im[:, :, :, None] * bb_re[None]
    to_rows = lambda a: a.transpose(1, 0, 3, 2).reshape(g, L * hch, p)
    min_re, min_im = to_rows(min_re), to_rows(min_im)
    to_cols = lambda a: a.transpose(1, 3, 0, 2).reshape(g, p, L * hch)
    mout_re, mout_im = to_cols(cp_re[1:]), to_cols(-cp_im[1:])

    def side_by_side_cols(a):
        z = jnp.zeros_like(a)
        even = jnp.concatenate([a, z], axis=-1)
        odd = jnp.concatenate([z, a], axis=-1)
        return jnp.where((jnp.arange(g) % 2 == 0)[:, None, None], even, odd)

    def side_by_side_rows(a):
        z = jnp.zeros_like(a)
        even = jnp.concatenate([a, z], axis=1)
        odd = jnp.concatenate([z, a], axis=1)
        return jnp.where((jnp.arange(g) % 2 == 0)[:, None, None], even, odd)

    n_steps = max(1, int(math.ceil(math.log2(rows_per_seq))))
    st_re, st_im = power(L * (2 ** jnp.arange(8)))
    pair_lanes = lambda a: a.reshape(8, g // 2, 2 * p).transpose(1, 0, 2)
    del n_steps
    return dict(
        s5_tz=tz.astype(BF16),
        s5_min_re=side_by_side_cols(min_re).astype(BF16), s5_min_im=side_by_side_cols(min_im).astype(BF16),
        s5_mout_re=side_by_side_rows(mout_re).astype(BF16), s5_mout_im=side_by_side_rows(mout_im).astype(BF16),
        s5_pw_re=pair_lanes(st_re), s5_pw_im=pair_lanes(st_im),
        s5_d=jnp.tile(d, (1, L)).reshape(g, 1, L * hch),
    )


def _layer_weights(l, p, rows_per_seq):
    w_in = p['w_in'][l]
    zeros = lambda n: jnp.zeros((D_MODEL, n), F32)
    kr = w_in[:, 384:416]
    w_mix = jnp.concatenate([
        w_in[:, 0:384],
        zeros(MLA_NOPE), kr, zeros(LANES - MLA_NOPE - MLA_ROPE),
        zeros(MLA_NOPE), _rot_half(kr), zeros(LANES - MLA_NOPE - MLA_ROPE),
        w_in[:, 416:1440],
        w_in[:, 1440:2208], w_in[:, 2208:2216], zeros(LANES - 2 * N_HEADS),
        w_in[:, 2216:2472]], axis=1)
    assert w_mix.shape[1] == W_MIX

    uq = p['mla_w_uq'][l].reshape(MLA_Q_RANK, N_HEADS, MLA_NOPE + MLA_ROPE)
    zq = jnp.zeros((MLA_Q_RANK, N_HEADS, LANES - MLA_NOPE - MLA_ROPE), F32)
    wq = jnp.concatenate([uq, zq], axis=-1).reshape(MLA_Q_RANK, N_HEADS * LANES)
    wqr = jnp.concatenate([jnp.zeros((MLA_Q_RANK, N_HEADS, MLA_NOPE), F32), _rot_half(uq[..., MLA_NOPE:]), zq],
                          axis=-1).reshape(MLA_Q_RANK, N_HEADS * LANES)
    ukv = p['mla_w_ukv'][l].reshape(MLA_KV_RANK, N_HEADS, MLA_NOPE + HEAD_W)
    zk = jnp.zeros((MLA_KV_RANK, N_HEADS, LANES - MLA_NOPE), F32)
    wk = jnp.concatenate([ukv[..., :MLA_NOPE], zk], axis=-1).reshape(MLA_KV_RANK, N_HEADS * LANES)
    wv = jnp.concatenate([ukv[..., MLA_NOPE:], zk], axis=-1).reshape(MLA_KV_RANK, N_HEADS * LANES)
    half = MLA_ROPE // 2
    inv_freq = ROPE_THETA ** (-np.arange(half, dtype=np.float32) / half)
    freq = np.zeros((1, LANES), np.float32)
    freq[0, MLA_NOPE:MLA_NOPE + half] = inv_freq
    freq[0, MLA_NOPE + half:MLA_NOPE + MLA_ROPE] = inv_freq

    spread = lambda b: jnp.repeat(b, HEAD_W)[None, :]
    row = lambda a: a[None, :]
    lw = dict(
        g_mix_pre=row(p['norm_mix_pre'][l]), g_mix_post=row(p['norm_mix_post'][l]),
        w_mix=w_mix.astype(BF16), w_gate=w_in[:, 2472:].astype(BF16),
        mla_q_norm=row(p['mla_q_norm'][l]), mla_kv_norm=row(p['mla_kv_norm'][l]),
        wq=wq.astype(BF16), wqr=wqr.astype(BF16), wk=wk.astype(BF16), wv=wv.astype(BF16),
        freq=jnp.asarray(freq),
        hg_out_norm=row(p['hg_out_norm'][l]),
        ml_conv_w=p['ml_conv_w'][l], ml_conv_b=row(p['ml_conv_b'][l]),
        ml_wq_bd=_block_diag(p['ml_w_q'][l]).astype(BF16), ml_wk_bd=_block_diag(p['ml_w_k'][l]).astype(BF16),
        ml_i_bias=spread(p['ml_i_bias'][l]), ml_f_bias=spread(p['ml_f_bias'][l]),
        ml_out_norm=row(p['ml_out_norm'][l]),
        s5_w_glu=p['s5_w_glu'][l].astype(BF16), s5_b_glu=row(p['s5_b_glu'][l]),
        w_branch=p['w_branch'][l].astype(BF16), w_out=p['w_out'][l].astype(BF16),
        g_xa_pre=row(p['norm_xa_pre'][l]), g_xa_post=row(p['norm_xa_post'][l]), g_mem=row(p['norm_mem'][l]),
        xa_wq=p['xa_wq'][l].astype(BF16), xa_wk=p['xa_wk'][l].astype(BF16),
        xa_wv=p['xa_wv'][l].astype(BF16), xa_wo=p['xa_wo'][l].astype(BF16),
        g_ffn_pre=row(p['norm_ffn_pre'][l]), g_ffn_post=row(p['norm_ffn_post'][l]),
        ffn_wa=p['ffn_w_in'][l][:, :D_FF].astype(BF16), ffn_wb=p['ffn_w_in'][l][:, D_FF:].astype(BF16),
        ffn_wo=p['ffn_w_out'][l].astype(BF16),
    )
    lw.update(_s5_tables(p['s5_a_re'][l], p['s5_a_im'][l], p['s5_log_dt'][l], p['s5_b_re'][l], p['s5_b_im'][l],
                         p['s5_c_re'][l], p['s5_c_im'][l], p['s5_d'][l], rows_per_seq))
    return lw


def kernel(x, mem, positions, norm_mix_pre, norm_mix_post, w_in, mla_q_norm, mla_w_uq, mla_kv_norm, mla_w_ukv, hg_lb_logits, hg_out_norm, ml_conv_w, ml_conv_b, ml_w_q, ml_w_k, ml_i_bias, ml_f_bias, ml_out_norm, s5_a_re, s5_a_im, s5_log_dt, s5_b_re, s5_b_im, s5_c_re, s5_c_im, s5_d, s5_w_glu, s5_b_glu, w_branch, w_out, norm_xa_pre, norm_xa_post, norm_mem, xa_wq, xa_wk, xa_wv, xa_wo, norm_ffn_pre, norm_ffn_post, ffn_w_in, ffn_w_out):
    p = dict(locals())
    batch, seq, _ = x.shape
    depth = w_in.shape[0]
    n = batch * seq
    tm = min(512, seq)
    tb = min(512, seq)
    blk = min(256, seq)
    x2 = x.reshape(n, D_MODEL)
    pos2 = positions.reshape(n, 1).astype(jnp.int32)
    for l in range(depth):
        lw = _layer_weights(l, p, seq // S5_CHUNK)
        q, k, v, pb, pc, pd = _front(x2, pos2, lw, tm)
        out_a = _mla(q, k, v, batch, seq, blk)
        out_b = _hgrn(pb, hg_lb_logits, lw['hg_out_norm'], l, batch, seq, tb)
        out_c = _mlstm(pc, lw, batch, seq, tb)
        y_s5 = _s5(pd, lw, batch, seq)
        x2 = _merge(x2, out_a, out_b, out_c, y_s5, lw, tm)
        mem_k, mem_v = _memkv(mem, lw)
        x2 = _xattn(x2, mem_k, mem_v, lw, batch, seq, tm)
        x2 = _ffn(x2, lw, tm)
    return x2.reshape(batch, seq, D_MODEL)
```

```python
import functools
import math

import numpy as np
import jax
import jax.numpy as jnp
from jax import lax
from jax.experimental import pallas as pl
from jax.experimental.pallas import tpu as pltpu

F32 = jnp.float32
BF16 = jnp.bfloat16

D_MODEL = 1024
NORM_EPS = 1e-6
MASK_VALUE = -1e30
GATE_FLOOR = 1e-30
BRANCH_W = 256
N_HEADS = 4
HEAD_W = 64
MLA_Q_RANK = 256
MLA_KV_RANK = 128
MLA_NOPE = 64
MLA_ROPE = 32
ROPE_THETA = 10000.0
ML_CONV = 4
S5_GROUPS = 16
S5_GROUP_CH = 16
S5_STATE = 64
S5_CHUNK = 16
XA_HEADS = 4
XA_DIM = 256
D_FF = 2816
FF_CHUNK = 256

LANES = 128
ATT_BLK = 256
CHUNK = 64
HG_CHUNK = 128
SUB = 16
VMEM_LIMIT = 56 * 1024 * 1024

OFF_AQ, OFF_AKV, OFF_KR, OFF_KRROT, W_SEC_A = 0, 256, 384, 512, 640
W_SEC_B = 4 * BRANCH_W
W_SEC_C = 3 * BRANCH_W + LANES
W_SEC_D = BRANCH_W
W_MIX = W_SEC_A + W_SEC_B + W_SEC_C + W_SEC_D


def _nt_dot(a, b):
    return lax.dot_general(a, b, (((1,), (1,)), ((), ())), preferred_element_type=F32)


def _tn_dot(a, b):
    return lax.dot_general(a, b, (((0,), (0,)), ((), ())), preferred_element_type=F32)


def _dot(a, b):
    return jnp.dot(a, b, preferred_element_type=F32)


def _dot_split(a, b_bf16, terms=3):
    acc = None
    rem = a
    for _ in range(terms):
        piece = rem.astype(BF16)
        part = _dot(piece, b_bf16)
        acc = part if acc is None else acc + part
        rem = rem - piece.astype(F32)
    return acc


def _rms(x, gain):
    return x * lax.rsqrt(jnp.mean(x * x, axis=-1, keepdims=True) + NORM_EPS) * gain


def _sigmoid(x):
    return 0.5 + 0.5 * jnp.tanh(0.5 * x)


def _silu(x):
    return x * _sigmoid(x)


def _log_sigmoid(x):
    return jnp.minimum(x, 0.0) - jnp.log(1.0 + jnp.exp(-jnp.abs(x)))


def _gelu_tanh(x):
    c = math.sqrt(2.0 / math.pi)
    return 0.5 * x * (1.0 + jnp.tanh(c * (x + 0.044715 * (x * x * x))))


def _head_block_ones(dtype):
    r = lax.broadcasted_iota(jnp.int32, (BRANCH_W, BRANCH_W), 0) // HEAD_W
    c = lax.broadcasted_iota(jnp.int32, (BRANCH_W, BRANCH_W), 1) // HEAD_W
    return (r == c).astype(dtype)


def _head_stack_mask(m, dtype):
    r = lax.broadcasted_iota(jnp.int32, (N_HEADS * m, BRANCH_W), 0) // m
    c = lax.broadcasted_iota(jnp.int32, (N_HEADS * m, BRANCH_W), 1) // HEAD_W
    return (r == c).astype(dtype)


def _head_norm(o, gain, ones_bd):
    msq = _dot_split(o * o, ones_bd, terms=2) * (1.0 / HEAD_W)
    return o * lax.rsqrt(msq + NORM_EPS) * gain


def _lower_tri(n, dtype):
    r = lax.broadcasted_iota(jnp.int32, (n, n), 0)
    c = lax.broadcasted_iota(jnp.int32, (n, n), 1)
    return (c <= r).astype(dtype)


def _front_kernel(x_ref, pos_ref, g_ref, wmix_ref, qn_ref, kvn_ref, wq_ref, wqr_ref, wk_ref, wvt_ref,
                  vones_ref, freq_ref, q_out, k_out, v_out, pb_out, pc_out, pd_lo_out, pd_hi_out):
    hn = _rms(x_ref[...], g_ref[...]).astype(BF16)
    pa = _dot(hn, wmix_ref[:, 0:W_SEC_A])
    pb_out[...] = _dot(hn, wmix_ref[:, W_SEC_A:W_SEC_A + W_SEC_B])
    pc_out[...] = _dot(hn, wmix_ref[:, W_SEC_A + W_SEC_B:W_SEC_A + W_SEC_B + W_SEC_C])
    pd = _dot(hn, wmix_ref[:, W_SEC_A + W_SEC_B + W_SEC_C:W_MIX])
    pd_lo_out[...] = pd[:, :LANES]
    pd_hi_out[...] = pd[:, LANES:]

    ang = pos_ref[...].astype(F32) * freq_ref[...]
    cos, sin = jnp.cos(ang), jnp.sin(ang)
    scale = (MLA_NOPE + MLA_ROPE) ** -0.5

    aqn = _rms(pa[:, OFF_AQ:OFF_AQ + MLA_Q_RANK], qn_ref[...]).astype(BF16)
    q0 = _dot(aqn, wq_ref[...])
    qr = _dot(aqn, wqr_ref[...])
    akvn = _rms(pa[:, OFF_AKV:OFF_AKV + MLA_KV_RANK], kvn_ref[...]).astype(BF16)
    kn = _dot(akvn, wk_ref[...])
    vt = _nt_dot(wvt_ref[...], akvn) + vones_ref[...]
    for t in range(v_out.shape[0]):
        v_out[t] = vt[:, t * ATT_BLK:(t + 1) * ATT_BLK].astype(v_out.dtype)
    k_rope = pa[:, OFF_KR:OFF_KR + LANES] * cos + pa[:, OFF_KRROT:OFF_KRROT + LANES] * sin
    for h in range(N_HEADS):
        sl = slice(h * LANES, (h + 1) * LANES)
        q_out[:, sl] = ((q0[:, sl] * cos + qr[:, sl] * sin) * scale).astype(q_out.dtype)
        k_out[:, sl] = (kn[:, sl] + k_rope).astype(k_out.dtype)


def _const_spec(shape):
    zeros = (0,) * len(shape)
    return pl.BlockSpec(shape, lambda *_: zeros, pipeline_mode=pl.Buffered(1))


def _front(x2, pos2, lw, tm):
    n = x2.shape[0]
    row = lambda w: pl.BlockSpec((tm, w), lambda i: (i, 0))
    consts = [lw['g_mix_pre'], lw['w_mix'], lw['mla_q_norm'], lw['mla_kv_norm'], lw['wq'], lw['wqr'],
              lw['wk'], lw['wv_t'], lw['v_ones'], lw['freq']]
    return pl.pallas_call(
        _front_kernel,
        grid=(n // tm,),
        in_specs=[row(D_MODEL), row(1)] + [_const_spec(c.shape) for c in consts],
        out_specs=[row(4 * LANES), row(4 * LANES),
                   pl.BlockSpec((tm // ATT_BLK, 4 * LANES, ATT_BLK), lambda i: (i, 0, 0)), row(W_SEC_B), row(W_SEC_C), row(LANES),
                   row(LANES)],
        out_shape=[jax.ShapeDtypeStruct((n, 4 * LANES), BF16), jax.ShapeDtypeStruct((n, 4 * LANES), BF16),
                   jax.ShapeDtypeStruct((n // ATT_BLK, 4 * LANES, ATT_BLK), BF16),
                   jax.ShapeDtypeStruct((n, W_SEC_B), F32),
                   jax.ShapeDtypeStruct((n, W_SEC_C), F32), jax.ShapeDtypeStruct((n, LANES), F32),
                   jax.ShapeDtypeStruct((n, LANES), F32)],
        compiler_params=pltpu.CompilerParams(dimension_semantics=("parallel",), vmem_limit_bytes=VMEM_LIMIT),
        name="front",
    )(x2, pos2, *consts)


def _mla_kernel(q_ref, k_ref, vt_ref, o_ref, *, blk):
    i = pl.program_id(1)
    key = lax.broadcasted_iota(jnp.int32, (blk, blk), 0)
    qry = lax.broadcasted_iota(jnp.int32, (blk, blk), 1)
    causal = key <= qry

    def step(j, carry, masked):
        start = pl.multiple_of(j * blk, blk)
        heads = [slice(h * LANES, (h + 1) * LANES) for h in range(N_HEADS)]
        scores = [_nt_dot(k_ref[pl.ds(start, blk), sl], q_ref[:, sl]) for sl in heads]
        if masked:
            scores = [jnp.where(causal, s, MASK_VALUE) for s in scores]
        m_new = [jnp.maximum(carry[h][0], jnp.max(scores[h], axis=0, keepdims=True)) for h in range(N_HEADS)]
        probs = [jnp.exp(scores[h] - m_new[h]).astype(BF16) for h in range(N_HEADS)]
        pv = [_dot(vt_ref[j, heads[h], :], probs[h]) for h in range(N_HEADS)]
        return tuple((m_new[h], jnp.exp(carry[h][0] - m_new[h]) * carry[h][1] + pv[h]) for h in range(N_HEADS))

    init = tuple((jnp.full((1, blk), MASK_VALUE, F32), jnp.zeros((LANES, blk), F32)) for _ in range(N_HEADS))
    carry = lax.fori_loop(0, i, functools.partial(step, masked=False), init)
    carry = step(i, carry, True)
    outs = [acc[:HEAD_W] * (1.0 / acc[HEAD_W:HEAD_W + 1]) for _, acc in carry]
    o_ref[...] = jnp.concatenate(outs, axis=0).T.astype(o_ref.dtype)


def _mla(q, k, vt, batch, seq):
    blk = ATT_BLK
    q3, k3 = (a.reshape(batch, seq, 4 * LANES) for a in (q, k))
    vt4 = vt.reshape(batch, seq // blk, 4 * LANES, blk)
    out = pl.pallas_call(
        functools.partial(_mla_kernel, blk=blk),
        grid=(batch, seq // blk),
        in_specs=[pl.BlockSpec((None, blk, 4 * LANES), lambda b, i: (b, i, 0)),
                  pl.BlockSpec((None, seq, 4 * LANES), lambda b, i: (b, 0, 0)),
                  pl.BlockSpec((None, seq // blk, 4 * LANES, blk), lambda b, i: (b, 0, 0, 0))],
        out_specs=pl.BlockSpec((None, blk, BRANCH_W), lambda b, i: (b, i, 0)),
        out_shape=jax.ShapeDtypeStruct((batch, seq, BRANCH_W), BF16),
        compiler_params=pltpu.CompilerParams(dimension_semantics=("parallel", "arbitrary"),
                                             vmem_limit_bytes=VMEM_LIMIT),
        name="mla",
    )(q3, k3, vt4)
    return out.reshape(batch * seq, BRANCH_W)


def _hgrn_kernel(pb_ref, lbl_ref, gain_ref, o_ref, st_ref, sh_ref, *, layer, n_chunks):
    @pl.when(pl.program_id(1) == 0)
    def _():
        st_ref[...] = jnp.zeros_like(st_ref)


    lg = lbl_ref[...]
    e = jnp.exp(lg - jnp.max(lg, axis=0, keepdims=True))
    sm = e / jnp.sum(e, axis=0, keepdims=True)
    lb = jnp.zeros((1, BRANCH_W), F32)
    for r in range(1, layer + 1):
        lb = lb + sm[r:r + 1, :]
    one_m_lb = 1.0 - lb
    log_one_m_lb = jnp.log(one_m_lb)

    ones_bd = _head_block_ones(BF16)
    ones_bd_f32 = _head_block_ones(F32)
    tri = _lower_tri(HG_CHUNK, BF16)
    row_in_sub = lax.broadcasted_iota(jnp.int32, (HG_CHUNK, BRANCH_W), 0) % SUB
    gain = gain_ref[...]

    def chunk_body(c, carry):
        start = pl.multiple_of(c * HG_CHUNK, HG_CHUNK)
        rows = pl.ds(start, HG_CHUNK)
        q = _silu(pb_ref[rows, 0:BRANCH_W])
        f_logit = pb_ref[rows, BRANCH_W:2 * BRANCH_W]
        v = pb_ref[rows, 2 * BRANCH_W:3 * BRANCH_W]
        e = jnp.exp(-jnp.abs(f_logit))
        r = 1.0 / (1.0 + e)
        nonneg = f_logit >= 0.0
        forget = lb + one_m_lb * jnp.where(nonneg, r, e * r)
        log_f = jnp.log(jnp.maximum(forget, GATE_FLOOR))
        k = one_m_lb * jnp.where(nonneg, e * r, r)
        b = _cumsum_rows(log_f, tri)
        b_end = b[HG_CHUNK - 1:HG_CHUNK, :]

        log_k = log_one_m_lb - jnp.maximum(f_logit, 0.0) + jnp.log(r)
        sh_ref[0] = log_k - b
        sh_ref[1] = v

        def key_row(a, s):
            return jnp.concatenate(
                [jnp.broadcast_to(sh_ref[a, blk * SUB + s:blk * SUB + s + 1, :], (SUB, BRANCH_W))
                 for blk in range(HG_CHUNK // SUB)], axis=0)

        o = jnp.zeros((HG_CHUNK, BRANCH_W), F32)
        group = 4
        for s0 in range(0, SUB, group):
            scores = []
            for s in range(s0, s0 + group):
                expo = b + key_row(0, s)
                if s > 0:
                    expo = jnp.where(row_in_sub >= s, expo, MASK_VALUE)
                scores.append(_dot((q * jnp.exp(expo)).astype(BF16), ones_bd))
            for s in range(s0, s0 + group):
                o = o + scores[s - s0] * key_row(1, s)

        m = SUB
        while m < HG_CHUNK:
            mask = _head_stack_mask(m, F32)
            parts = []
            for lo in range(0, HG_CHUNK, 2 * m):
                left = slice(lo, lo + m)
                right = slice(lo + m, lo + 2 * m)
                ref = b[lo + m - 1:lo + m, :]
                kt = k[left] * jnp.exp(ref - b[left])
                qt = q[right] * jnp.exp(b[right] - ref)
                k_hat = (jnp.concatenate([kt] * N_HEADS, axis=0) * mask).astype(BF16)
                v_hat = (jnp.concatenate([v[left]] * N_HEADS, axis=0) * mask).astype(BF16)
                a = _nt_dot(qt.astype(BF16), k_hat)
                parts.append(jnp.zeros((m, BRANCH_W), F32))
                parts.append(_dot(a.astype(BF16), v_hat))
            o = o + jnp.concatenate(parts, axis=0)
            m *= 2

        st = st_ref[...]
        o = o + _nt_dot((q * jnp.exp(b)).astype(BF16), st.astype(BF16))
        k_end = k * jnp.exp(b_end - b)
        st_ref[...] = (st * jnp.exp(b_end) + _tn_dot(v.astype(BF16), k_end.astype(BF16))) * ones_bd_f32

        gate = _silu(pb_ref[rows, 3 * BRANCH_W:4 * BRANCH_W])
        o_ref[rows, :] = (_head_norm(o, gain, ones_bd) * gate).astype(o_ref.dtype)
        return carry

    lax.fori_loop(0, n_chunks, chunk_body, 0, unroll=2)


def _cumsum_rows(x, tri):
    acc = None
    rem = x
    for _ in range(2):
        piece = rem.astype(BF16)
        part = _dot(tri, piece)
        acc = part if acc is None else acc + part
        rem = rem - piece.astype(F32)
    return acc


def _hgrn(pb, lb_logits, gain, layer, batch, seq, tb):
    pb3 = pb.reshape(batch, seq, W_SEC_B)
    out = pl.pallas_call(
        functools.partial(_hgrn_kernel, layer=layer, n_chunks=tb // HG_CHUNK),
        grid=(batch, seq // tb),
        in_specs=[pl.BlockSpec((None, tb, W_SEC_B), lambda b, i: (b, i, 0)),
                  _const_spec(lb_logits.shape), _const_spec(gain.shape)],
        out_specs=pl.BlockSpec((None, tb, BRANCH_W), lambda b, i: (b, i, 0)),
        out_shape=jax.ShapeDtypeStruct((batch, seq, BRANCH_W), BF16),
        scratch_shapes=[pltpu.VMEM((BRANCH_W, BRANCH_W), F32), pltpu.VMEM((2, HG_CHUNK, BRANCH_W), F32)],
        compiler_params=pltpu.CompilerParams(dimension_semantics=("parallel", "arbitrary"),
                                             vmem_limit_bytes=VMEM_LIMIT),
        name="hgrn",
    )(pb3, lb_logits, gain)
    return out.reshape(batch * seq, BRANCH_W)


def _mlstm_kernel(pc_ref, cw_ref, cb_ref, wq_ref, wk_ref, ib_ref, fb_ref, gain_ref, o_ref,
                  xext_ref, q_ref, k_ref, c_ref, n_ref, m_ref, *, tb):
    pad = 8

    @pl.when(pl.program_id(1) == 0)
    def _():
        xext_ref[0:pad, :] = jnp.zeros((pad, BRANCH_W), F32)
        c_ref[...] = jnp.zeros_like(c_ref)
        n_ref[...] = jnp.zeros_like(n_ref)
        m_ref[...] = jnp.zeros_like(m_ref)

    xext_ref[pad:pad + tb, :] = pc_ref[:, 0:BRANCH_W]
    conv = jnp.zeros((tb, BRANCH_W), F32) + cb_ref[...]
    for j in range(ML_CONV):
        conv = conv + xext_ref[pad - (ML_CONV - 1) + j:pad - (ML_CONV - 1) + j + tb, :] * cw_ref[j:j + 1, :]
    xext_ref[0:pad, :] = xext_ref[tb:tb + pad, :]
    xc = _silu(conv).astype(BF16)
    q_ref[...] = _dot(xc, wq_ref[...])
    k_ref[...] = _dot(xc, wk_ref[...]) * (HEAD_W ** -0.5)

    ones_bd = _head_block_ones(BF16)
    ones_bd_f32 = _head_block_ones(F32)
    tri = _lower_tri(CHUNK, BF16)
    stack_mask = _head_stack_mask(CHUNK, F32)
    lane = lax.broadcasted_iota(jnp.int32, (CHUNK, BRANCH_W), 1)
    rowi = lax.broadcasted_iota(jnp.int32, (CHUNK, BRANCH_W), 0)
    diag_sel = (lane % HEAD_W == rowi).astype(F32)
    causal = (lane % HEAD_W) <= rowi
    gr = lax.broadcasted_iota(jnp.int32, (LANES, BRANCH_W), 0)
    gc = lax.broadcasted_iota(jnp.int32, (LANES, BRANCH_W), 1) // HEAD_W
    spread_i = (gr == gc).astype(BF16)
    spread_f = (gr == gc + N_HEADS).astype(BF16)
    gain = gain_ref[...]

    def head_max(a):
        out = jnp.zeros_like(a)
        for h in range(N_HEADS):
            mx = jnp.max(a[:, h * HEAD_W:(h + 1) * HEAD_W], axis=-1, keepdims=True)
            out = jnp.where(lane // HEAD_W == h, mx, out)
        return out

    def chunk_body(c, carry):
        rows = pl.ds(pl.multiple_of(c * CHUNK, CHUNK), CHUNK)
        q = q_ref[rows, :]
        k = k_ref[rows, :]
        v = pc_ref[rows, BRANCH_W:2 * BRANCH_W]
        gates = pc_ref[rows, 3 * BRANCH_W:3 * BRANCH_W + LANES]
        i_log = _dot_split(gates, spread_i) + ib_ref[...]
        f_log = _log_sigmoid(_dot_split(gates, spread_f) + fb_ref[...])
        b = _cumsum_rows(f_log, tri)
        b_end = b[CHUNK - 1:CHUNK, :]
        b_row = jnp.sum(b * diag_sel, axis=0, keepdims=True)
        i_row = jnp.sum(i_log * diag_sel, axis=0, keepdims=True)
        c0, n0, m0 = c_ref[...], n_ref[...], m_ref[...]

        dmat = jnp.where(causal, b - b_row + i_row, MASK_VALUE)
        a = b + m0
        m_t = jnp.maximum(a, head_max(dmat))
        w_intra = jnp.exp(dmat - m_t)
        w_inter = jnp.exp(a - m_t)
        qb = q.astype(BF16)
        k_hat = (jnp.concatenate([k] * N_HEADS, axis=0) * stack_mask).astype(BF16)
        v_hat = (jnp.concatenate([v] * N_HEADS, axis=0) * stack_mask).astype(BF16)
        qk = (_nt_dot(qb, k_hat) * w_intra).astype(BF16)
        num = _dot(qk, v_hat) + w_inter * _dot(qb, c0.astype(BF16))
        den = _dot(qk, ones_bd) + w_inter * _dot((q * n0).astype(BF16), ones_bd)
        h_out = num / jnp.maximum(jnp.abs(den), jnp.exp(-m_t))

        g_end = b_end - b + i_log
        m_new = jnp.maximum(b_end + m0, jnp.max(g_end, axis=0, keepdims=True))
        keep = jnp.exp(b_end + m0 - m_new)
        kw = k * jnp.exp(g_end - m_new)
        c_ref[...] = (keep * c0 + _tn_dot(kw.astype(BF16), v.astype(BF16))) * ones_bd_f32
        n_ref[...] = keep * n0 + jnp.sum(kw, axis=0, keepdims=True)
        m_ref[...] = m_new

        out_gate = _sigmoid(pc_ref[rows, 2 * BRANCH_W:3 * BRANCH_W])
        o_ref[rows, :] = (out_gate * _head_norm(h_out, gain, ones_bd)).astype(o_ref.dtype)
        return carry

    lax.fori_loop(0, tb // CHUNK, chunk_body, 0, unroll=2)


def _mlstm(pc, lw, batch, seq, tb):
    pc3 = pc.reshape(batch, seq, W_SEC_C)
    consts = [lw['ml_conv_w'], lw['ml_conv_b'], lw['ml_wq_bd'], lw['ml_wk_bd'], lw['ml_i_bias'],
              lw['ml_f_bias'], lw['ml_out_norm']]
    out = pl.pallas_call(
        functools.partial(_mlstm_kernel, tb=tb),
        grid=(batch, seq // tb),
        in_specs=[pl.BlockSpec((None, tb, W_SEC_C), lambda b, i: (b, i, 0))]
                 + [_const_spec(c.shape) for c in consts],
        out_specs=pl.BlockSpec((None, tb, BRANCH_W), lambda b, i: (b, i, 0)),
        out_shape=jax.ShapeDtypeStruct((batch, seq, BRANCH_W), BF16),
        scratch_shapes=[pltpu.VMEM((tb + 8, BRANCH_W), F32), pltpu.VMEM((tb, BRANCH_W), F32),
                        pltpu.VMEM((tb, BRANCH_W), F32), pltpu.VMEM((BRANCH_W, BRANCH_W), F32),
                        pltpu.VMEM((1, BRANCH_W), F32), pltpu.VMEM((1, BRANCH_W), F32)],
        compiler_params=pltpu.CompilerParams(dimension_semantics=("parallel", "arbitrary"),
                                             vmem_limit_bytes=VMEM_LIMIT),
        name="mlstm",
    )(pc3, *consts)
    return out.reshape(batch * seq, BRANCH_W)


def _s5_kernel(u_lo_ref, u_hi_ref, tz_ref, min_re_ref, min_im_ref, mout_re_ref, mout_im_ref, pw_re_ref,
               pw_im_ref, d_ref, y_lo_ref, y_hi_ref, ug_ref, yl_ref, *, n_rows):
    gw = S5_GROUP_CH
    half = S5_GROUPS // 2
    for l in range(S5_CHUNK):
        for part, u_ref in enumerate((u_lo_ref, u_hi_ref)):
            x_l = u_ref[pl.ds(l, n_rows, stride=S5_CHUNK), :]
            for g in range(half):
                ug_ref[part * half + g, :, l * gw:(l + 1) * gw] = x_l[:, g * gw:(g + 1) * gw]

    row = lax.broadcasted_iota(jnp.int32, (n_rows, LANES), 0)
    for pair in range(S5_GROUPS // 2):
        u = [ug_ref[2 * pair + s] for s in range(2)]
        ub = [a.astype(BF16) for a in u]
        s_re = _dot(ub[0], min_re_ref[2 * pair]) + _dot(ub[1], min_re_ref[2 * pair + 1])
        s_im = _dot(ub[0], min_im_ref[2 * pair]) + _dot(ub[1], min_im_ref[2 * pair + 1])
        step, k = 1, 0
        while step < n_rows:
            keep = row >= step
            p_re = jnp.where(keep, pltpu.roll(s_re, step, 0), 0.0)
            p_im = jnp.where(keep, pltpu.roll(s_im, step, 0), 0.0)
            a_re = pw_re_ref[pair, k:k + 1, :]
            a_im = pw_im_ref[pair, k:k + 1, :]
            s_re, s_im = s_re + a_re * p_re - a_im * p_im, s_im + a_re * p_im + a_im * p_re
            step, k = step * 2, k + 1
        first = row >= 1
        s0_re = jnp.where(first, pltpu.roll(s_re, 1, 0), 0.0).astype(BF16)
        s0_im = jnp.where(first, pltpu.roll(s_im, 1, 0), 0.0).astype(BF16)
        for s in range(2):
            g = 2 * pair + s
            y = (_dot(ub[s], tz_ref[g]) + _dot(s0_re, mout_re_ref[g]) + _dot(s0_im, mout_im_ref[g])
                 + u[s] * d_ref[g])
            for l in range(S5_CHUNK):
                yl_ref[l, :, g * gw:(g + 1) * gw] = y[:, l * gw:(l + 1) * gw]

    for l in range(S5_CHUNK):
        y_lo_ref[pl.ds(l, n_rows, stride=S5_CHUNK), :] = yl_ref[l, :, :LANES]
        y_hi_ref[pl.ds(l, n_rows, stride=S5_CHUNK), :] = yl_ref[l, :, LANES:]


def _s5(pd_lo, pd_hi, lw, batch, seq):
    n_rows = seq // S5_CHUNK
    width = S5_CHUNK * S5_GROUP_CH
    consts = [lw['s5_tz'], lw['s5_min_re'], lw['s5_min_im'], lw['s5_mout_re'], lw['s5_mout_im'],
              lw['s5_pw_re'], lw['s5_pw_im'], lw['s5_d']]
    seq_blk = pl.BlockSpec((None, seq, LANES), lambda b: (b, 0, 0))
    out = jax.ShapeDtypeStruct((batch, seq, LANES), F32)
    y_lo, y_hi = pl.pallas_call(
        functools.partial(_s5_kernel, n_rows=n_rows),
        grid=(batch,),
        in_specs=[seq_blk, seq_blk] + [_const_spec(c.shape) for c in consts],
        out_specs=[seq_blk, seq_blk],
        out_shape=[out, out],
        scratch_shapes=[pltpu.VMEM((S5_GROUPS, n_rows, width), F32), pltpu.VMEM((S5_CHUNK, n_rows, BRANCH_W), F32)],
        compiler_params=pltpu.CompilerParams(dimension_semantics=("parallel",), vmem_limit_bytes=VMEM_LIMIT),
        name="s5",
    )(pd_lo.reshape(batch, seq, LANES), pd_hi.reshape(batch, seq, LANES), *consts)
    return y_lo.reshape(batch * seq, LANES), y_hi.reshape(batch * seq, LANES)


def _merge_kernel(x_ref, a_ref, b_ref, c_ref, y_lo_ref, y_hi_ref, gpre_ref, wgate_ref, wbr_ref, wout_ref,
                  wglu_ref, bglu_ref, gpost_ref, o_ref):
    x = x_ref[...]
    hn = _rms(x, gpre_ref[...]).astype(BF16)
    y = _gelu_tanh(jnp.concatenate([y_lo_ref[...], y_hi_ref[...]], axis=-1))
    out_d = (y * _sigmoid(_dot(y.astype(BF16), wglu_ref[...]) + bglu_ref[...])).astype(BF16)
    branches = (a_ref[...], b_ref[...], c_ref[...], out_d)
    merged = None
    for n in range(4):
        gate = _sigmoid(_dot(hn, wgate_ref[:, n * D_MODEL:(n + 1) * D_MODEL]))
        term = gate * _dot(branches[n], wbr_ref[n])
        merged = term if merged is None else merged + term
    mix = _dot(merged.astype(BF16), wout_ref[...])
    o_ref[...] = x + _rms(mix, gpost_ref[...])


def _merge(x2, out_a, out_b, out_c, y_lo, y_hi, lw, tm):
    n = x2.shape[0]
    row = lambda w: pl.BlockSpec((tm, w), lambda i: (i, 0))
    consts = [lw['g_mix_pre'], lw['w_gate'], lw['w_branch'], lw['w_out'], lw['s5_w_glu'], lw['s5_b_glu'],
              lw['g_mix_post']]
    return pl.pallas_call(
        _merge_kernel,
        grid=(n // tm,),
        in_specs=[row(D_MODEL), row(BRANCH_W), row(BRANCH_W), row(BRANCH_W), row(LANES), row(LANES)]
                 + [_const_spec(c.shape) for c in consts],
        out_specs=row(D_MODEL),
        out_shape=jax.ShapeDtypeStruct((n, D_MODEL), F32),
        compiler_params=pltpu.CompilerParams(dimension_semantics=("parallel",), vmem_limit_bytes=VMEM_LIMIT),
        name="merge",
    )(x2, out_a, out_b, out_c, y_lo, y_hi, *consts)


def _memkv_kernel(mem_ref, g_ref, wk_ref, wv_ref, k_out, v_out):
    mn = _rms(mem_ref[...], g_ref[...]).astype(BF16)
    k_out[...] = _dot(mn, wk_ref[...]).astype(k_out.dtype)
    v_out[...] = _dot(mn, wv_ref[...]).astype(v_out.dtype)


def _memkv(mem, lw):
    batch, mlen, _ = mem.shape
    consts = [lw['g_mem'], lw['xa_wk'], lw['xa_wv']]
    blk = pl.BlockSpec((None, mlen, D_MODEL), lambda b: (b, 0, 0))
    return pl.pallas_call(
        _memkv_kernel,
        grid=(batch,),
        in_specs=[blk] + [_const_spec(c.shape) for c in consts],
        out_specs=[blk, blk],
        out_shape=[jax.ShapeDtypeStruct((batch, mlen, D_MODEL), BF16)] * 2,
        compiler_params=pltpu.CompilerParams(dimension_semantics=("parallel",), vmem_limit_bytes=VMEM_LIMIT),
        name="memkv",
    )(mem, *consts)


def _xattn_kernel(x_ref, k_ref, v_ref, gpre_ref, wq_ref, wo_ref, gpost_ref, o_ref):
    x = x_ref[...]
    hn = _rms(x, gpre_ref[...]).astype(BF16)
    q = (_dot(hn, wq_ref[...]) * (XA_DIM ** -0.5)).astype(BF16)
    outs = []
    for h in range(XA_HEADS):
        sl = slice(h * XA_DIM, (h + 1) * XA_DIM)
        s = _nt_dot(q[:, sl], k_ref[:, sl])
        p = jnp.exp(s - jnp.max(s, axis=-1, keepdims=True))
        inv = 1.0 / jnp.sum(p, axis=-1, keepdims=True)
        outs.append((_dot(p.astype(BF16), v_ref[:, sl]) * inv).astype(BF16))
    xa = _dot(jnp.concatenate(outs, axis=-1), wo_ref[...])
    o_ref[...] = x + _rms(xa, gpost_ref[...])


def _xattn(x2, mem_k, mem_v, lw, batch, seq, tm):
    mlen = mem_k.shape[1]
    x3 = x2.reshape(batch, seq, D_MODEL)
    consts = [lw['g_xa_pre'], lw['xa_wq'], lw['xa_wo'], lw['g_xa_post']]
    row = pl.BlockSpec((None, tm, D_MODEL), lambda b, i: (b, i, 0))
    kv = pl.BlockSpec((None, mlen, D_MODEL), lambda b, i: (b, 0, 0))
    out = pl.pallas_call(
        _xattn_kernel,
        grid=(batch, seq // tm),
        in_specs=[row, kv, kv] + [_const_spec(c.shape) for c in consts],
        out_specs=row,
        out_shape=jax.ShapeDtypeStruct((batch, seq, D_MODEL), F32),
        compiler_params=pltpu.CompilerParams(dimension_semantics=("parallel", "parallel"),
                                             vmem_limit_bytes=VMEM_LIMIT),
        name="xattn",
    )(x3, mem_k, mem_v, *consts)
    return out.reshape(batch * seq, D_MODEL)


def _ffn_kernel(x_ref, gpre_ref, wa_ref, wb_ref, wo_ref, gpost_ref, o_ref):
    x = x_ref[...]
    hn = _rms(x, gpre_ref[...]).astype(BF16)
    acc = None
    for j in range(D_FF // FF_CHUNK):
        sl = slice(j * FF_CHUNK, (j + 1) * FF_CHUNK)
        act = (_silu(_dot(hn, wa_ref[:, sl])) * _dot(hn, wb_ref[:, sl])).astype(BF16)
        part = _dot(act, wo_ref[sl, :])
        acc = part if acc is None else acc + part
    o_ref[...] = x + _rms(acc, gpost_ref[...])


def _ffn(x2, lw, tm):
    n = x2.shape[0]
    row = pl.BlockSpec((tm, D_MODEL), lambda i: (i, 0))
    consts = [lw['g_ffn_pre'], lw['ffn_wa'], lw['ffn_wb'], lw['ffn_wo'], lw['g_ffn_post']]
    return pl.pallas_call(
        _ffn_kernel,
        grid=(n // tm,),
        in_specs=[row] + [_const_spec(c.shape) for c in consts],
        out_specs=row,
        out_shape=jax.ShapeDtypeStruct((n, D_MODEL), F32),
        compiler_params=pltpu.CompilerParams(dimension_semantics=("parallel",), vmem_limit_bytes=VMEM_LIMIT),
        name="ffn",
    )(x2, *consts)


def _rot_half(w):
    half = MLA_ROPE // 2
    return jnp.concatenate([-w[..., half:], w[..., :half]], axis=-1)


def _block_diag(w):
    h, d, e = w.shape
    eye = jnp.eye(h, dtype=w.dtype)
    return (eye[:, None, :, None] * w[:, :, None, :]).reshape(h * d, h * e)


def _s5_tables(a_re, a_im, log_dt, b_re, b_im, c_re, c_im, d, rows_per_seq):
    hi = lax.Precision.HIGHEST
    g, p, hch = b_re.shape
    L = S5_CHUNK
    dt = jnp.exp(log_dt)[:, None]
    lam_re, lam_im = a_re * dt, a_im * dt
    mag = jnp.exp(lam_re)
    ab_re, ab_im = mag * jnp.cos(lam_im), mag * jnp.sin(lam_im)
    inv_abs2 = 1.0 / (a_re * a_re + a_im * a_im)
    z_re = ((ab_re - 1.0) * a_re + ab_im * a_im) * inv_abs2
    z_im = (ab_im * a_re - (ab_re - 1.0) * a_im) * inv_abs2
    bb_re = z_re[..., None] * b_re - z_im[..., None] * b_im
    bb_im = z_re[..., None] * b_im + z_im[..., None] * b_re

    def power(k):
        k = jnp.asarray(k, F32)[..., None, None]
        m = jnp.exp(k * lam_re)
        return m * jnp.cos(k * lam_im), m * jnp.sin(k * lam_im)

    pw_re, pw_im = power(jnp.arange(L + 1))
    cp_re = c_re[None] * pw_re[:, :, None, :] - c_im[None] * pw_im[:, :, None, :]
    cp_im = c_re[None] * pw_im[:, :, None, :] + c_im[None] * pw_re[:, :, None, :]
    kern = (jnp.einsum('kgap,gph->gkah', cp_re, bb_re, precision=hi)
            - jnp.einsum('kgap,gph->gkah', cp_im, bb_im, precision=hi))
    lag = jnp.arange(L)[None, :] - jnp.arange(L)[:, None]
    tz = jnp.where((lag >= 0)[None, :, :, None, None], kern[:, jnp.clip(lag, 0, L)], 0.0)
    tz = tz.transpose(0, 1, 4, 2, 3).reshape(g, L * hch, L * hch)

    rp_re, rp_im = pw_re[L - 1 - jnp.arange(L)], pw_im[L - 1 - jnp.arange(L)]
    min_re = rp_re[:, :, :, None] * bb_re[None] - rp_im[:, :, :, None] * bb_im[None]
    min_im = rp_re[:, :, :, None] * bb_im[None] + rp_im[:, :, :, None] * bb_re[None]
    to_rows = lambda a: a.transpose(1, 0, 3, 2).reshape(g, L * hch, p)
    min_re, min_im = to_rows(min_re), to_rows(min_im)
    to_cols = lambda a: a.transpose(1, 3, 0, 2).reshape(g, p, L * hch)
    mout_re, mout_im = to_cols(cp_re[1:]), to_cols(-cp_im[1:])

    def side_by_side_cols(a):
        z = jnp.zeros_like(a)
        even = jnp.concatenate([a, z], axis=-1)
        odd = jnp.concatenate([z, a], axis=-1)
        return jnp.where((jnp.arange(g) % 2 == 0)[:, None, None], even, odd)

    def side_by_side_rows(a):
        z = jnp.zeros_like(a)
        even = jnp.concatenate([a, z], axis=1)
        odd = jnp.concatenate([z, a], axis=1)
        return jnp.where((jnp.arange(g) % 2 == 0)[:, None, None], even, odd)

    n_steps = max(1, int(math.ceil(math.log2(rows_per_seq))))
    st_re, st_im = power(L * (2 ** jnp.arange(8)))
    pair_lanes = lambda a: a.reshape(8, g // 2, 2 * p).transpose(1, 0, 2)
    del n_steps
    return dict(
        s5_tz=tz.astype(BF16),
        s5_min_re=side_by_side_cols(min_re).astype(BF16), s5_min_im=side_by_side_cols(min_im).astype(BF16),
        s5_mout_re=side_by_side_rows(mout_re).astype(BF16), s5_mout_im=side_by_side_rows(mout_im).astype(BF16),
        s5_pw_re=pair_lanes(st_re), s5_pw_im=pair_lanes(st_im),
        s5_d=jnp.tile(d, (1, L)).reshape(g, 1, L * hch),
    )


def _layer_weights(l, p, rows_per_seq):
    w_in = p['w_in'][l]
    zeros = lambda n: jnp.zeros((D_MODEL, n), F32)
    kr = w_in[:, 384:416]
    w_mix = jnp.concatenate([
        w_in[:, 0:384],
        zeros(MLA_NOPE), kr, zeros(LANES - MLA_NOPE - MLA_ROPE),
        zeros(MLA_NOPE), _rot_half(kr), zeros(LANES - MLA_NOPE - MLA_ROPE),
        w_in[:, 416:1440],
        w_in[:, 1440:2208], w_in[:, 2208:2216], zeros(LANES - 2 * N_HEADS),
        w_in[:, 2216:2472]], axis=1)
    assert w_mix.shape[1] == W_MIX

    uq = p['mla_w_uq'][l].reshape(MLA_Q_RANK, N_HEADS, MLA_NOPE + MLA_ROPE)
    zq = jnp.zeros((MLA_Q_RANK, N_HEADS, LANES - MLA_NOPE - MLA_ROPE), F32)
    wq = jnp.concatenate([uq, zq], axis=-1).reshape(MLA_Q_RANK, N_HEADS * LANES)
    wqr = jnp.concatenate([jnp.zeros((MLA_Q_RANK, N_HEADS, MLA_NOPE), F32), _rot_half(uq[..., MLA_NOPE:]), zq],
                          axis=-1).reshape(MLA_Q_RANK, N_HEADS * LANES)
    ukv = p['mla_w_ukv'][l].reshape(MLA_KV_RANK, N_HEADS, MLA_NOPE + HEAD_W)
    zk = jnp.zeros((MLA_KV_RANK, N_HEADS, LANES - MLA_NOPE), F32)
    wk = jnp.concatenate([ukv[..., :MLA_NOPE], zk], axis=-1).reshape(MLA_KV_RANK, N_HEADS * LANES)
    wv_t = jnp.concatenate([ukv[..., MLA_NOPE:], zk], axis=-1).reshape(MLA_KV_RANK, N_HEADS * LANES).T
    v_ones = np.zeros((N_HEADS * LANES, 1), np.float32)
    v_ones[HEAD_W::LANES] = 1.0
    half = MLA_ROPE // 2
    inv_freq = ROPE_THETA ** (-np.arange(half, dtype=np.float32) / half)
    freq = np.zeros((1, LANES), np.float32)
    freq[0, MLA_NOPE:MLA_NOPE + half] = inv_freq
    freq[0, MLA_NOPE + half:MLA_NOPE + MLA_ROPE] = inv_freq

    spread = lambda b: jnp.repeat(b, HEAD_W)[None, :]
    row = lambda a: a[None, :]
    lw = dict(
        g_mix_pre=row(p['norm_mix_pre'][l]), g_mix_post=row(p['norm_mix_post'][l]),
        w_mix=w_mix.astype(BF16), w_gate=w_in[:, 2472:].astype(BF16),
        mla_q_norm=row(p['mla_q_norm'][l]), mla_kv_norm=row(p['mla_kv_norm'][l]),
        wq=wq.astype(BF16), wqr=wqr.astype(BF16), wk=wk.astype(BF16), wv_t=wv_t.astype(BF16), v_ones=jnp.asarray(v_ones),
        freq=jnp.asarray(freq),
        hg_out_norm=row(p['hg_out_norm'][l]),
        ml_conv_w=p['ml_conv_w'][l], ml_conv_b=row(p['ml_conv_b'][l]),
        ml_wq_bd=_block_diag(p['ml_w_q'][l]).astype(BF16), ml_wk_bd=_block_diag(p['ml_w_k'][l]).astype(BF16),
        ml_i_bias=spread(p['ml_i_bias'][l]), ml_f_bias=spread(p['ml_f_bias'][l]),
        ml_out_norm=row(p['ml_out_norm'][l]),
        s5_w_glu=p['s5_w_glu'][l].astype(BF16), s5_b_glu=row(p['s5_b_glu'][l]),
        w_branch=p['w_branch'][l].astype(BF16), w_out=p['w_out'][l].astype(BF16),
        g_xa_pre=row(p['norm_xa_pre'][l]), g_xa_post=row(p['norm_xa_post'][l]), g_mem=row(p['norm_mem'][l]),
        xa_wq=p['xa_wq'][l].astype(BF16), xa_wk=p['xa_wk'][l].astype(BF16),
        xa_wv=p['xa_wv'][l].astype(BF16), xa_wo=p['xa_wo'][l].astype(BF16),
        g_ffn_pre=row(p['norm_ffn_pre'][l]), g_ffn_post=row(p['norm_ffn_post'][l]),
        ffn_wa=p['ffn_w_in'][l][:, :D_FF].astype(BF16), ffn_wb=p['ffn_w_in'][l][:, D_FF:].astype(BF16),
        ffn_wo=p['ffn_w_out'][l].astype(BF16),
    )
    lw.update(_s5_tables(p['s5_a_re'][l], p['s5_a_im'][l], p['s5_log_dt'][l], p['s5_b_re'][l], p['s5_b_im'][l],
                         p['s5_c_re'][l], p['s5_c_im'][l], p['s5_d'][l], rows_per_seq))
    return lw


def kernel(x, mem, positions, norm_mix_pre, norm_mix_post, w_in, mla_q_norm, mla_w_uq, mla_kv_norm, mla_w_ukv, hg_lb_logits, hg_out_norm, ml_conv_w, ml_conv_b, ml_w_q, ml_w_k, ml_i_bias, ml_f_bias, ml_out_norm, s5_a_re, s5_a_im, s5_log_dt, s5_b_re, s5_b_im, s5_c_re, s5_c_im, s5_d, s5_w_glu, s5_b_glu, w_branch, w_out, norm_xa_pre, norm_xa_post, norm_mem, xa_wq, xa_wk, xa_wv, xa_wo, norm_ffn_pre, norm_ffn_post, ffn_w_in, ffn_w_out):
    p = dict(locals())
    batch, seq, _ = x.shape
    depth = w_in.shape[0]
    n = batch * seq
    tm = min(512, seq)
    tb = min(512, seq)
    x2 = x.reshape(n, D_MODEL)
    pos2 = positions.reshape(n, 1).astype(jnp.int32)
    for l in range(depth):
        lw = _layer_weights(l, p, seq // S5_CHUNK)
        q, k, vt, pb, pc, pd_lo, pd_hi = _front(x2, pos2, lw, tm)
        out_a = _mla(q, k, vt, batch, seq)
        out_b = _hgrn(pb, hg_lb_logits, lw['hg_out_norm'], l, batch, seq, tb)
        out_c = _mlstm(pc, lw, batch, seq, tb)
        y_lo, y_hi = _s5(pd_lo, pd_hi, lw, batch, seq)
        x2 = _merge(x2, out_a, out_b, out_c, y_lo, y_hi, lw, tm)
        mem_k, mem_v = _memkv(mem, lw)
        x2 = _xattn(x2, mem_k, mem_v, lw, batch, seq, tm)
        x2 = _ffn(x2, lw, tm)
    return x2.reshape(batch, seq, D_MODEL)
```

```python
import functools
import math
from typing import NamedTuple

import numpy as np
import jax
import jax.numpy as jnp
from jax import lax
from jax.experimental import pallas as pl
from jax.experimental.pallas import tpu as pltpu

F32 = jnp.float32
BF16 = jnp.bfloat16

D_MODEL = 1024
NORM_EPS = 1e-6
MASK_VALUE = -1e30
GATE_FLOOR = 1e-30
BRANCH_W = 256
N_HEADS = 4
HEAD_W = 64
MLA_Q_RANK = 256
MLA_KV_RANK = 128
MLA_NOPE = 64
MLA_ROPE = 32
ROPE_THETA = 10000.0
ML_CONV = 4
S5_GROUPS = 16
S5_GROUP_CH = 16
S5_STATE = 64
S5_CHUNK = 16
S5_SCAN_STEPS = 8
XA_HEADS = 4
XA_DIM = 256
D_FF = 2816
FF_CHUNK = 256

LANES = 128
ATT_BLK = 256
CHUNK = 64
HG_CHUNK = 128
SUB = 16
ROW_TILE = 512
VMEM_LIMIT = 56 * 1024 * 1024

OFF_AQ, OFF_AKV, OFF_KR, OFF_KRROT, W_SEC_A = 0, 256, 384, 512, 640
W_SEC_B = 4 * BRANCH_W
W_SEC_C = 3 * BRANCH_W + LANES
W_SEC_D = BRANCH_W
W_MIX = W_SEC_A + W_SEC_B + W_SEC_C + W_SEC_D
IN_KR, IN_B, IN_C, IN_CG, IN_D, IN_GATE = 384, 416, 1440, 2208, 2216, 2472


def _nt_dot(a, b):
    return lax.dot_general(a, b, (((1,), (1,)), ((), ())), preferred_element_type=F32)


def _tn_dot(a, b):
    return lax.dot_general(a, b, (((0,), (0,)), ((), ())), preferred_element_type=F32)


def _dot(a, b):
    return jnp.dot(a, b, preferred_element_type=F32)


def _dot_split(a, b_bf16, terms=3):
    acc = None
    rem = a
    for _ in range(terms):
        piece = rem.astype(BF16)
        part = _dot(piece, b_bf16)
        acc = part if acc is None else acc + part
        rem = rem - piece.astype(F32)
    return acc


def _cumsum_rows(x, tri):
    acc = None
    rem = x
    for _ in range(2):
        piece = rem.astype(BF16)
        part = _dot(tri, piece)
        acc = part if acc is None else acc + part
        rem = rem - piece.astype(F32)
    return acc


def _rms(x, gain):
    return x * lax.rsqrt(jnp.mean(x * x, axis=-1, keepdims=True) + NORM_EPS) * gain


def _sigmoid(x):
    return 0.5 + 0.5 * jnp.tanh(0.5 * x)


def _silu(x):
    return x * _sigmoid(x)


def _log_sigmoid(x):
    return jnp.minimum(x, 0.0) - jnp.log(1.0 + jnp.exp(-jnp.abs(x)))


def _gelu_tanh(x):
    c = math.sqrt(2.0 / math.pi)
    return 0.5 * x * (1.0 + jnp.tanh(c * (x + 0.044715 * (x * x * x))))


def _head_block_ones(dtype):
    r = lax.broadcasted_iota(jnp.int32, (BRANCH_W, BRANCH_W), 0) // HEAD_W
    c = lax.broadcasted_iota(jnp.int32, (BRANCH_W, BRANCH_W), 1) // HEAD_W
    return (r == c).astype(dtype)


def _head_stack_mask(m, dtype):
    r = lax.broadcasted_iota(jnp.int32, (N_HEADS * m, BRANCH_W), 0) // m
    c = lax.broadcasted_iota(jnp.int32, (N_HEADS * m, BRANCH_W), 1) // HEAD_W
    return (r == c).astype(dtype)


def _head_norm(o, gain, ones_bd):
    msq = _dot_split(o * o, ones_bd, terms=2) * (1.0 / HEAD_W)
    return o * lax.rsqrt(msq + NORM_EPS) * gain


def _chunk_lower_tri(n, chunk, dtype):
    r = lax.broadcasted_iota(jnp.int32, (n, n), 0)
    c = lax.broadcasted_iota(jnp.int32, (n, n), 1)
    return ((c <= r) & (r // chunk == c // chunk)).astype(dtype)


class _LayerParam(NamedTuple):
    array: jax.Array
    layer: int


def _const_spec(shape):
    zeros = (0,) * len(shape)
    return pl.BlockSpec(shape, lambda *_: zeros, pipeline_mode=pl.Buffered(1))


def _param_specs(consts):
    specs, operands = [], []
    for c in consts:
        if isinstance(c, _LayerParam):
            tail = c.array.shape[1:]
            idx = (c.layer,) + (0,) * len(tail)
            specs.append(pl.BlockSpec((None,) + tail, lambda *_, idx=idx: idx, pipeline_mode=pl.Buffered(1)))
            operands.append(c.array)
        else:
            specs.append(_const_spec(c.shape))
            operands.append(c)
    return specs, operands


def _front_kernel(x_ref, pos_ref, g_ref, wmix_ref, qn_ref, kvn_ref, wq_ref, wqr_ref, wk_ref, wvt_ref,
                  vones_ref, freq_ref, q_out, k_out, v_out, pb_out, pc_out, pd_lo_out, pd_hi_out):
    hn = _rms(x_ref[...], g_ref[...]).astype(BF16)
    pa = _dot(hn, wmix_ref[:, 0:W_SEC_A])
    pb_out[...] = _dot(hn, wmix_ref[:, W_SEC_A:W_SEC_A + W_SEC_B])
    pc_out[...] = _dot(hn, wmix_ref[:, W_SEC_A + W_SEC_B:W_SEC_A + W_SEC_B + W_SEC_C])
    pd = _dot(hn, wmix_ref[:, W_SEC_A + W_SEC_B + W_SEC_C:W_MIX])
    pd_lo_out[...] = pd[:, :LANES]
    pd_hi_out[...] = pd[:, LANES:]

    ang = pos_ref[...].astype(F32) * freq_ref[...]
    cos, sin = jnp.cos(ang), jnp.sin(ang)
    scale = (MLA_NOPE + MLA_ROPE) ** -0.5

    aqn = _rms(pa[:, OFF_AQ:OFF_AQ + MLA_Q_RANK], qn_ref[...]).astype(BF16)
    q0 = _dot(aqn, wq_ref[...])
    qr = _dot(aqn, wqr_ref[...])
    akvn = _rms(pa[:, OFF_AKV:OFF_AKV + MLA_KV_RANK], kvn_ref[...]).astype(BF16)
    kn = _dot(akvn, wk_ref[...])
    vt = _nt_dot(wvt_ref[...], akvn) + vones_ref[...]
    for t in range(v_out.shape[0]):
        v_out[t] = vt[:, t * ATT_BLK:(t + 1) * ATT_BLK].astype(v_out.dtype)
    k_rope = pa[:, OFF_KR:OFF_KR + LANES] * cos + pa[:, OFF_KRROT:OFF_KRROT + LANES] * sin
    for h in range(N_HEADS):
        sl = slice(h * LANES, (h + 1) * LANES)
        q_out[:, sl] = ((q0[:, sl] * cos + qr[:, sl] * sin) * scale).astype(q_out.dtype)
        k_out[:, sl] = (kn[:, sl] + k_rope).astype(k_out.dtype)


def _front(x2, pos2, lw, tm):
    n = x2.shape[0]
    row = lambda w: pl.BlockSpec((tm, w), lambda i: (i, 0))
    specs, consts = _param_specs([lw['g_mix_pre'], lw['w_mix'], lw['mla_q_norm'], lw['mla_kv_norm'], lw['wq'],
                                  lw['wqr'], lw['wk'], lw['wv_t'], lw['v_ones'], lw['freq']])
    return pl.pallas_call(
        _front_kernel,
        grid=(n // tm,),
        in_specs=[row(D_MODEL), row(1)] + specs,
        out_specs=[row(4 * LANES), row(4 * LANES),
                   pl.BlockSpec((tm // ATT_BLK, 4 * LANES, ATT_BLK), lambda i: (i, 0, 0)), row(W_SEC_B),
                   row(W_SEC_C), row(LANES), row(LANES)],
        out_shape=[jax.ShapeDtypeStruct((n, 4 * LANES), BF16), jax.ShapeDtypeStruct((n, 4 * LANES), BF16),
                   jax.ShapeDtypeStruct((n // ATT_BLK, 4 * LANES, ATT_BLK), BF16),
                   jax.ShapeDtypeStruct((n, W_SEC_B), F32),
                   jax.ShapeDtypeStruct((n, W_SEC_C), F32), jax.ShapeDtypeStruct((n, LANES), F32),
                   jax.ShapeDtypeStruct((n, LANES), F32)],
        compiler_params=pltpu.CompilerParams(dimension_semantics=("parallel",), vmem_limit_bytes=VMEM_LIMIT),
        name="front",
    )(x2, pos2, *consts)


def _mla_kernel(q_ref, k_ref, vt_ref, o_ref, *, blk):
    i = pl.program_id(1)
    key = lax.broadcasted_iota(jnp.int32, (blk, blk), 0)
    qry = lax.broadcasted_iota(jnp.int32, (blk, blk), 1)
    causal = key <= qry

    def step(j, carry, masked):
        start = pl.multiple_of(j * blk, blk)
        heads = [slice(h * LANES, (h + 1) * LANES) for h in range(N_HEADS)]
        scores = [_nt_dot(k_ref[pl.ds(start, blk), sl], q_ref[:, sl]) for sl in heads]
        if masked:
            scores = [jnp.where(causal, s, MASK_VALUE) for s in scores]
        m_new = [jnp.maximum(carry[h][0], jnp.max(scores[h], axis=0, keepdims=True)) for h in range(N_HEADS)]
        probs = [jnp.exp(scores[h] - m_new[h]).astype(BF16) for h in range(N_HEADS)]
        pv = [_dot(vt_ref[j, heads[h], :], probs[h]) for h in range(N_HEADS)]
        return tuple((m_new[h], jnp.exp(carry[h][0] - m_new[h]) * carry[h][1] + pv[h]) for h in range(N_HEADS))

    init = tuple((jnp.full((1, blk), MASK_VALUE, F32), jnp.zeros((LANES, blk), F32)) for _ in range(N_HEADS))
    carry = lax.fori_loop(0, i, functools.partial(step, masked=False), init)
    carry = step(i, carry, True)
    outs = [acc[:HEAD_W] * (1.0 / acc[HEAD_W:HEAD_W + 1]) for _, acc in carry]
    o_ref[...] = jnp.concatenate(outs, axis=0).T.astype(o_ref.dtype)


def _mla(q, k, vt, batch, seq):
    blk = ATT_BLK
    q3, k3 = (a.reshape(batch, seq, 4 * LANES) for a in (q, k))
    vt4 = vt.reshape(batch, seq // blk, 4 * LANES, blk)
    out = pl.pallas_call(
        functools.partial(_mla_kernel, blk=blk),
        grid=(batch, seq // blk),
        in_specs=[pl.BlockSpec((None, blk, 4 * LANES), lambda b, i: (b, i, 0)),
                  pl.BlockSpec((None, seq, 4 * LANES), lambda b, i: (b, 0, 0)),
                  pl.BlockSpec((None, seq // blk, 4 * LANES, blk), lambda b, i: (b, 0, 0, 0))],
        out_specs=pl.BlockSpec((None, blk, BRANCH_W), lambda b, i: (b, i, 0)),
        out_shape=jax.ShapeDtypeStruct((batch, seq, BRANCH_W), BF16),
        compiler_params=pltpu.CompilerParams(dimension_semantics=("parallel", "arbitrary"),
                                             vmem_limit_bytes=VMEM_LIMIT),
        name="mla",
    )(q3, k3, vt4)
    return out.reshape(batch * seq, BRANCH_W)


def _hgrn_kernel(pb_ref, lbl_ref, gain_ref, o_ref, st_ref, q_sc, k_sc, b_sc, lk_sc, o_sc, *, layer, tb):
    @pl.when(pl.program_id(1) == 0)
    def _():
        st_ref[...] = jnp.zeros_like(st_ref)

    lg = lbl_ref[...]
    e = jnp.exp(lg - jnp.max(lg, axis=0, keepdims=True))
    sm = e / jnp.sum(e, axis=0, keepdims=True)
    lb = jnp.zeros((1, BRANCH_W), F32)
    for r in range(1, layer + 1):
        lb = lb + sm[r:r + 1, :]
    one_m_lb = 1.0 - lb

    ones_bd = _head_block_ones(BF16)
    ones_bd_f32 = _head_block_ones(F32)
    row_in_sub = lax.broadcasted_iota(jnp.int32, (HG_CHUNK, BRANCH_W), 0) % SUB

    f_logit = pb_ref[:, BRANCH_W:2 * BRANCH_W]
    e = jnp.exp(-jnp.abs(f_logit))
    r = 1.0 / (1.0 + e)
    nonneg = f_logit >= 0.0
    forget = lb + one_m_lb * jnp.where(nonneg, r, e * r)
    b_all = _cumsum_rows(jnp.log(jnp.maximum(forget, GATE_FLOOR)), _chunk_lower_tri(tb, HG_CHUNK, BF16))
    b_sc[...] = b_all
    k_sc[...] = one_m_lb * jnp.where(nonneg, e * r, r)
    lk_sc[...] = jnp.log(one_m_lb) - jnp.maximum(f_logit, 0.0) + jnp.log(r) - b_all
    q_sc[...] = _silu(pb_ref[:, 0:BRANCH_W])

    def chunk_body(c, carry):
        start = pl.multiple_of(c * HG_CHUNK, HG_CHUNK)
        rows = pl.ds(start, HG_CHUNK)
        q, k, b = q_sc[rows, :], k_sc[rows, :], b_sc[rows, :]
        v = pb_ref[rows, 2 * BRANCH_W:3 * BRANCH_W]
        b_end = b[HG_CHUNK - 1:HG_CHUNK, :]

        st = st_ref[...]
        o = _nt_dot((q * jnp.exp(b)).astype(BF16), st.astype(BF16))
        k_end = k * jnp.exp(b_end - b)
        st_ref[...] = (st * jnp.exp(b_end) + _tn_dot(v.astype(BF16), k_end.astype(BF16))) * ones_bd_f32

        m = SUB
        while m < HG_CHUNK:
            mask = _head_stack_mask(m, F32)
            parts = []
            for lo in range(0, HG_CHUNK, 2 * m):
                left = slice(lo, lo + m)
                right = slice(lo + m, lo + 2 * m)
                ref = b[lo + m - 1:lo + m, :]
                kt = k[left] * jnp.exp(ref - b[left])
                qt = q[right] * jnp.exp(b[right] - ref)
                k_hat = (jnp.concatenate([kt] * N_HEADS, axis=0) * mask).astype(BF16)
                v_hat = (jnp.concatenate([v[left]] * N_HEADS, axis=0) * mask).astype(BF16)
                a = _nt_dot(qt.astype(BF16), k_hat)
                parts.append(jnp.zeros((m, BRANCH_W), F32))
                parts.append(_dot(a.astype(BF16), v_hat))
            o = o + jnp.concatenate(parts, axis=0)
            m *= 2

        def key_row(ref, col, s):
            return jnp.concatenate(
                [jnp.broadcast_to(ref[pl.ds(start + blk * SUB + s, 1), col:col + BRANCH_W], (SUB, BRANCH_W))
                 for blk in range(HG_CHUNK // SUB)], axis=0)

        for s in range(SUB):
            expo = b + key_row(lk_sc, 0, s)
            if s > 0:
                expo = jnp.where(row_in_sub >= s, expo, MASK_VALUE)
            score = _dot((q * jnp.exp(expo)).astype(BF16), ones_bd)
            o = o + score * key_row(pb_ref, 2 * BRANCH_W, s)
        o_sc[rows, :] = o
        return carry

    lax.fori_loop(0, tb // HG_CHUNK, chunk_body, 0, unroll=2)

    gate = _silu(pb_ref[:, 3 * BRANCH_W:4 * BRANCH_W])
    o_ref[...] = (_head_norm(o_sc[...], gain_ref[...], ones_bd) * gate).astype(o_ref.dtype)


def _hgrn(pb, lb_logits, gain, layer, batch, seq, tb):
    pb3 = pb.reshape(batch, seq, W_SEC_B)
    specs, consts = _param_specs([lb_logits, gain])
    blk_scratch = pltpu.VMEM((tb, BRANCH_W), F32)
    out = pl.pallas_call(
        functools.partial(_hgrn_kernel, layer=layer, tb=tb),
        grid=(batch, seq // tb),
        in_specs=[pl.BlockSpec((None, tb, W_SEC_B), lambda b, i: (b, i, 0))] + specs,
        out_specs=pl.BlockSpec((None, tb, BRANCH_W), lambda b, i: (b, i, 0)),
        out_shape=jax.ShapeDtypeStruct((batch, seq, BRANCH_W), BF16),
        scratch_shapes=[pltpu.VMEM((BRANCH_W, BRANCH_W), F32)] + [blk_scratch] * 5,
        compiler_params=pltpu.CompilerParams(dimension_semantics=("parallel", "arbitrary"),
                                             vmem_limit_bytes=VMEM_LIMIT),
        name="hgrn",
    )(pb3, *consts)
    return out.reshape(batch * seq, BRANCH_W)


def _mlstm_kernel(pc_ref, cw_ref, cb_ref, wq_ref, wk_ref, ib_ref, fb_ref, gain_ref, o_ref,
                  xext_ref, q_sc, k_sc, il_sc, b_sc, h_sc, c_ref, n_ref, m_ref, *, tb):
    pad = 8

    @pl.when(pl.program_id(1) == 0)
    def _():
        xext_ref[0:pad, :] = jnp.zeros((pad, BRANCH_W), F32)
        c_ref[...] = jnp.zeros_like(c_ref)
        n_ref[...] = jnp.zeros_like(n_ref)
        m_ref[...] = jnp.zeros_like(m_ref)

    xext_ref[pad:pad + tb, :] = pc_ref[:, 0:BRANCH_W]
    conv = jnp.zeros((tb, BRANCH_W), F32) + cb_ref[...]
    for j in range(ML_CONV):
        conv = conv + xext_ref[pad - (ML_CONV - 1) + j:pad - (ML_CONV - 1) + j + tb, :] * cw_ref[j:j + 1, :]
    xext_ref[0:pad, :] = xext_ref[tb:tb + pad, :]
    xc = _silu(conv).astype(BF16)
    q_sc[...] = _dot(xc, wq_ref[...])
    k_sc[...] = _dot(xc, wk_ref[...]) * (HEAD_W ** -0.5)

    gr = lax.broadcasted_iota(jnp.int32, (LANES, BRANCH_W), 0)
    gc = lax.broadcasted_iota(jnp.int32, (LANES, BRANCH_W), 1) // HEAD_W
    gates = pc_ref[:, 3 * BRANCH_W:3 * BRANCH_W + LANES]
    il_sc[...] = _dot_split(gates, (gr == gc).astype(BF16)) + ib_ref[...]
    f_log = _log_sigmoid(_dot_split(gates, (gr == gc + N_HEADS).astype(BF16)) + fb_ref[...])
    b_sc[...] = _cumsum_rows(f_log, _chunk_lower_tri(tb, CHUNK, BF16))

    ones_bd = _head_block_ones(BF16)
    ones_bd_f32 = _head_block_ones(F32)
    stack_mask = _head_stack_mask(CHUNK, F32)
    lane = lax.broadcasted_iota(jnp.int32, (CHUNK, BRANCH_W), 1)
    rowi = lax.broadcasted_iota(jnp.int32, (CHUNK, BRANCH_W), 0)
    diag_sel = (lane % HEAD_W == rowi).astype(F32)
    causal = (lane % HEAD_W) <= rowi

    def head_max(a):
        out = jnp.zeros_like(a)
        for h in range(N_HEADS):
            mx = jnp.max(a[:, h * HEAD_W:(h + 1) * HEAD_W], axis=-1, keepdims=True)
            out = jnp.where(lane // HEAD_W == h, mx, out)
        return out

    def chunk_body(c, carry):
        rows = pl.ds(pl.multiple_of(c * CHUNK, CHUNK), CHUNK)
        q, k, i_log, b = q_sc[rows, :], k_sc[rows, :], il_sc[rows, :], b_sc[rows, :]
        v = pc_ref[rows, BRANCH_W:2 * BRANCH_W]
        qb = q.astype(BF16)
        k_hat = (jnp.concatenate([k] * N_HEADS, axis=0) * stack_mask).astype(BF16)
        v_hat = (jnp.concatenate([v] * N_HEADS, axis=0) * stack_mask).astype(BF16)
        scores = _nt_dot(qb, k_hat)
        c0, n0, m0 = c_ref[...], n_ref[...], m_ref[...]
        inter_num = _dot(qb, c0.astype(BF16))
        inter_den = _dot((q * n0).astype(BF16), ones_bd)

        b_end = b[CHUNK - 1:CHUNK, :]
        b_row = jnp.sum(b * diag_sel, axis=0, keepdims=True)
        i_row = jnp.sum(i_log * diag_sel, axis=0, keepdims=True)
        dmat = jnp.where(causal, b - b_row + i_row, MASK_VALUE)
        a = b + m0
        m_t = jnp.maximum(a, head_max(dmat))
        w_inter = jnp.exp(a - m_t)
        qk = (scores * jnp.exp(dmat - m_t)).astype(BF16)
        num = _dot(qk, v_hat) + w_inter * inter_num
        den = _dot(qk, ones_bd) + w_inter * inter_den
        h_sc[rows, :] = num / jnp.maximum(jnp.abs(den), jnp.exp(-m_t))

        g_end = b_end - b + i_log
        m_new = jnp.maximum(b_end + m0, jnp.max(g_end, axis=0, keepdims=True))
        keep = jnp.exp(b_end + m0 - m_new)
        kw = k * jnp.exp(g_end - m_new)
        c_ref[...] = (keep * c0 + _tn_dot(kw.astype(BF16), v.astype(BF16))) * ones_bd_f32
        n_ref[...] = keep * n0 + jnp.sum(kw, axis=0, keepdims=True)
        m_ref[...] = m_new
        return carry

    lax.fori_loop(0, tb // CHUNK, chunk_body, 0, unroll=2)

    out_gate = _sigmoid(pc_ref[:, 2 * BRANCH_W:3 * BRANCH_W])
    o_ref[...] = (out_gate * _head_norm(h_sc[...], gain_ref[...], ones_bd)).astype(o_ref.dtype)


def _mlstm(pc, lw, batch, seq, tb):
    pc3 = pc.reshape(batch, seq, W_SEC_C)
    specs, consts = _param_specs([lw['ml_conv_w'], lw['ml_conv_b'], lw['ml_wq_bd'], lw['ml_wk_bd'],
                                  lw['ml_i_bias'], lw['ml_f_bias'], lw['ml_out_norm']])
    blk_scratch = pltpu.VMEM((tb, BRANCH_W), F32)
    out = pl.pallas_call(
        functools.partial(_mlstm_kernel, tb=tb),
        grid=(batch, seq // tb),
        in_specs=[pl.BlockSpec((None, tb, W_SEC_C), lambda b, i: (b, i, 0))] + specs,
        out_specs=pl.BlockSpec((None, tb, BRANCH_W), lambda b, i: (b, i, 0)),
        out_shape=jax.ShapeDtypeStruct((batch, seq, BRANCH_W), BF16),
        scratch_shapes=[pltpu.VMEM((tb + 8, BRANCH_W), F32)] + [blk_scratch] * 5
                       + [pltpu.VMEM((BRANCH_W, BRANCH_W), F32), pltpu.VMEM((1, BRANCH_W), F32),
                          pltpu.VMEM((1, BRANCH_W), F32)],
        compiler_params=pltpu.CompilerParams(dimension_semantics=("parallel", "arbitrary"),
                                             vmem_limit_bytes=VMEM_LIMIT),
        name="mlstm",
    )(pc3, *consts)
    return out.reshape(batch * seq, BRANCH_W)


def _s5_kernel(u_lo_ref, u_hi_ref, tz_ref, min_re_ref, min_im_ref, mout_re_ref, mout_im_ref, pw_re_ref,
               pw_im_ref, d_ref, y_lo_ref, y_hi_ref, ug_ref, yl_ref, *, n_rows):
    gw = S5_GROUP_CH
    half = S5_GROUPS // 2
    for l in range(S5_CHUNK):
        for part, u_ref in enumerate((u_lo_ref, u_hi_ref)):
            x_l = u_ref[pl.ds(l, n_rows, stride=S5_CHUNK), :]
            for g in range(half):
                ug_ref[part * half + g, :, l * gw:(l + 1) * gw] = x_l[:, g * gw:(g + 1) * gw]

    row = lax.broadcasted_iota(jnp.int32, (n_rows, LANES), 0)
    for pair in range(S5_GROUPS // 2):
        u = [ug_ref[2 * pair + s] for s in range(2)]
        ub = [a.astype(BF16) for a in u]
        s_re = _dot(ub[0], min_re_ref[2 * pair]) + _dot(ub[1], min_re_ref[2 * pair + 1])
        s_im = _dot(ub[0], min_im_ref[2 * pair]) + _dot(ub[1], min_im_ref[2 * pair + 1])
        step, k = 1, 0
        while step < n_rows:
            keep = row >= step
            p_re = jnp.where(keep, pltpu.roll(s_re, step, 0), 0.0)
            p_im = jnp.where(keep, pltpu.roll(s_im, step, 0), 0.0)
            a_re = pw_re_ref[pair, k:k + 1, :]
            a_im = pw_im_ref[pair, k:k + 1, :]
            s_re, s_im = s_re + a_re * p_re - a_im * p_im, s_im + a_re * p_im + a_im * p_re
            step, k = step * 2, k + 1
        first = row >= 1
        s0_re = jnp.where(first, pltpu.roll(s_re, 1, 0), 0.0).astype(BF16)
        s0_im = jnp.where(first, pltpu.roll(s_im, 1, 0), 0.0).astype(BF16)
        for s in range(2):
            g = 2 * pair + s
            y = (_dot(ub[s], tz_ref[g]) + _dot(s0_re, mout_re_ref[g]) + _dot(s0_im, mout_im_ref[g])
                 + u[s] * d_ref[g])
            for l in range(S5_CHUNK):
                yl_ref[l, :, g * gw:(g + 1) * gw] = y[:, l * gw:(l + 1) * gw]

    for l in range(S5_CHUNK):
        y_lo_ref[pl.ds(l, n_rows, stride=S5_CHUNK), :] = yl_ref[l, :, :LANES]
        y_hi_ref[pl.ds(l, n_rows, stride=S5_CHUNK), :] = yl_ref[l, :, LANES:]


def _s5(pd_lo, pd_hi, lw, batch, seq):
    n_rows = seq // S5_CHUNK
    assert n_rows <= 2 ** S5_SCAN_STEPS
    width = S5_CHUNK * S5_GROUP_CH
    specs, consts = _param_specs([lw['s5_tz'], lw['s5_min_re'], lw['s5_min_im'], lw['s5_mout_re'],
                                  lw['s5_mout_im'], lw['s5_pw_re'], lw['s5_pw_im'], lw['s5_d']])
    seq_blk = pl.BlockSpec((None, seq, LANES), lambda b: (b, 0, 0))
    out = jax.ShapeDtypeStruct((batch, seq, LANES), F32)
    y_lo, y_hi = pl.pallas_call(
        functools.partial(_s5_kernel, n_rows=n_rows),
        grid=(batch,),
        in_specs=[seq_blk, seq_blk] + specs,
        out_specs=[seq_blk, seq_blk],
        out_shape=[out, out],
        scratch_shapes=[pltpu.VMEM((S5_GROUPS, n_rows, width), F32), pltpu.VMEM((S5_CHUNK, n_rows, BRANCH_W), F32)],
        compiler_params=pltpu.CompilerParams(dimension_semantics=("parallel",), vmem_limit_bytes=VMEM_LIMIT),
        name="s5",
    )(pd_lo.reshape(batch, seq, LANES), pd_hi.reshape(batch, seq, LANES), *consts)
    return y_lo.reshape(batch * seq, LANES), y_hi.reshape(batch * seq, LANES)


def _merge_kernel(x_ref, a_ref, b_ref, c_ref, y_lo_ref, y_hi_ref, gpre_ref, wgate_ref, wbr_ref, wout_ref,
                  wglu_ref, bglu_ref, gpost_ref, o_ref):
    x = x_ref[...]
    hn = _rms(x, gpre_ref[...]).astype(BF16)
    y = _gelu_tanh(jnp.concatenate([y_lo_ref[...], y_hi_ref[...]], axis=-1))
    out_d = (y * _sigmoid(_dot(y.astype(BF16), wglu_ref[...]) + bglu_ref[...])).astype(BF16)
    branches = (a_ref[...], b_ref[...], c_ref[...], out_d)
    merged = None
    for n in range(4):
        gate = _sigmoid(_dot(hn, wgate_ref[:, n * D_MODEL:(n + 1) * D_MODEL]))
        term = gate * _dot(branches[n], wbr_ref[n])
        merged = term if merged is None else merged + term
    mix = _dot(merged.astype(BF16), wout_ref[...])
    o_ref[...] = x + _rms(mix, gpost_ref[...])


def _merge(x2, out_a, out_b, out_c, y_lo, y_hi, lw, tm):
    n = x2.shape[0]
    row = lambda w: pl.BlockSpec((tm, w), lambda i: (i, 0))
    specs, consts = _param_specs([lw['g_mix_pre'], lw['w_gate'], lw['w_branch'], lw['w_out'], lw['s5_w_glu'],
                                  lw['s5_b_glu'], lw['g_mix_post']])
    return pl.pallas_call(
        _merge_kernel,
        grid=(n // tm,),
        in_specs=[row(D_MODEL), row(BRANCH_W), row(BRANCH_W), row(BRANCH_W), row(LANES), row(LANES)] + specs,
        out_specs=row(D_MODEL),
        out_shape=jax.ShapeDtypeStruct((n, D_MODEL), F32),
        compiler_params=pltpu.CompilerParams(dimension_semantics=("parallel",), vmem_limit_bytes=VMEM_LIMIT),
        name="merge",
    )(x2, out_a, out_b, out_c, y_lo, y_hi, *consts)


def _memkv_kernel(mem_ref, g_ref, wk_ref, wv_ref, k_out, v_out):
    mn = _rms(mem_ref[...], g_ref[...]).astype(BF16)
    k_out[...] = _dot(mn, wk_ref[...]).astype(k_out.dtype)
    v_out[...] = _dot(mn, wv_ref[...]).astype(v_out.dtype)


def _memkv(mem, lw):
    batch, mlen, _ = mem.shape
    specs, consts = _param_specs([lw['g_mem'], lw['xa_wk'], lw['xa_wv']])
    blk = pl.BlockSpec((None, mlen, D_MODEL), lambda b: (b, 0, 0))
    return pl.pallas_call(
        _memkv_kernel,
        grid=(batch,),
        in_specs=[blk] + specs,
        out_specs=[blk, blk],
        out_shape=[jax.ShapeDtypeStruct((batch, mlen, D_MODEL), BF16)] * 2,
        compiler_params=pltpu.CompilerParams(dimension_semantics=("parallel",), vmem_limit_bytes=VMEM_LIMIT),
        name="memkv",
    )(mem, *consts)


def _xattn_kernel(x_ref, k_ref, v_ref, gpre_ref, wq_ref, wo_ref, gpost_ref, o_ref):
    x = x_ref[...]
    hn = _rms(x, gpre_ref[...]).astype(BF16)
    q = (_dot(hn, wq_ref[...]) * (XA_DIM ** -0.5)).astype(BF16)
    outs = []
    for h in range(XA_HEADS):
        sl = slice(h * XA_DIM, (h + 1) * XA_DIM)
        s = _nt_dot(q[:, sl], k_ref[:, sl])
        p = jnp.exp(s - jnp.max(s, axis=-1, keepdims=True))
        inv = 1.0 / jnp.sum(p, axis=-1, keepdims=True)
        outs.append((_dot(p.astype(BF16), v_ref[:, sl]) * inv).astype(BF16))
    xa = _dot(jnp.concatenate(outs, axis=-1), wo_ref[...])
    o_ref[...] = x + _rms(xa, gpost_ref[...])


def _xattn(x2, mem_k, mem_v, lw, batch, seq, tm):
    mlen = mem_k.shape[1]
    x3 = x2.reshape(batch, seq, D_MODEL)
    specs, consts = _param_specs([lw['g_xa_pre'], lw['xa_wq'], lw['xa_wo'], lw['g_xa_post']])
    row = pl.BlockSpec((None, tm, D_MODEL), lambda b, i: (b, i, 0))
    kv = pl.BlockSpec((None, mlen, D_MODEL), lambda b, i: (b, 0, 0))
    out = pl.pallas_call(
        _xattn_kernel,
        grid=(batch, seq // tm),
        in_specs=[row, kv, kv] + specs,
        out_specs=row,
        out_shape=jax.ShapeDtypeStruct((batch, seq, D_MODEL), F32),
        compiler_params=pltpu.CompilerParams(dimension_semantics=("parallel", "parallel"),
                                             vmem_limit_bytes=VMEM_LIMIT),
        name="xattn",
    )(x3, mem_k, mem_v, *consts)
    return out.reshape(batch * seq, D_MODEL)


def _ffn_kernel(x_ref, gpre_ref, win_ref, wo_ref, gpost_ref, o_ref):
    x = x_ref[...]
    hn = _rms(x, gpre_ref[...]).astype(BF16)
    acc = None
    for j in range(D_FF // FF_CHUNK):
        lo = j * FF_CHUNK
        a = _dot(hn, win_ref[:, lo:lo + FF_CHUNK])
        b = _dot(hn, win_ref[:, D_FF + lo:D_FF + lo + FF_CHUNK])
        part = _dot((_silu(a) * b).astype(BF16), wo_ref[lo:lo + FF_CHUNK, :])
        acc = part if acc is None else acc + part
    o_ref[...] = x + _rms(acc, gpost_ref[...])


def _ffn(x2, lw, tm):
    n = x2.shape[0]
    row = pl.BlockSpec((tm, D_MODEL), lambda i: (i, 0))
    specs, consts = _param_specs([lw['g_ffn_pre'], lw['ffn_w_in'], lw['ffn_wo'], lw['g_ffn_post']])
    return pl.pallas_call(
        _ffn_kernel,
        grid=(n // tm,),
        in_specs=[row] + specs,
        out_specs=row,
        out_shape=jax.ShapeDtypeStruct((n, D_MODEL), F32),
        compiler_params=pltpu.CompilerParams(dimension_semantics=("parallel",), vmem_limit_bytes=VMEM_LIMIT),
        name="ffn",
    )(x2, *consts)


def _rot_half(w):
    half = MLA_ROPE // 2
    return jnp.concatenate([-w[..., half:], w[..., :half]], axis=-1)


def _block_diag(w):
    depth, h, d, e = w.shape
    eye = jnp.eye(h, dtype=w.dtype)
    return (eye[None, :, None, :, None] * w[:, :, :, None, :]).reshape(depth, h * d, h * e)


def _s5_tables(a_re, a_im, log_dt, b_re, b_im, c_re, c_im, d):
    hi = lax.Precision.HIGHEST
    g, p, hch = b_re.shape
    L = S5_CHUNK
    dt = jnp.exp(log_dt)[:, None]
    lam_re, lam_im = a_re * dt, a_im * dt
    mag = jnp.exp(lam_re)
    ab_re, ab_im = mag * jnp.cos(lam_im), mag * jnp.sin(lam_im)
    inv_abs2 = 1.0 / (a_re * a_re + a_im * a_im)
    z_re = ((ab_re - 1.0) * a_re + ab_im * a_im) * inv_abs2
    z_im = (ab_im * a_re - (ab_re - 1.0) * a_im) * inv_abs2
    bb_re = z_re[..., None] * b_re - z_im[..., None] * b_im
    bb_im = z_re[..., None] * b_im + z_im[..., None] * b_re

    def power(k):
        k = jnp.asarray(k, F32)[..., None, None]
        m = jnp.exp(k * lam_re)
        return m * jnp.cos(k * lam_im), m * jnp.sin(k * lam_im)

    pw_re, pw_im = power(jnp.arange(L + 1))
    cp_re = c_re[None] * pw_re[:, :, None, :] - c_im[None] * pw_im[:, :, None, :]
    cp_im = c_re[None] * pw_im[:, :, None, :] + c_im[None] * pw_re[:, :, None, :]
    kern = (jnp.einsum('kgap,gph->gkah', cp_re, bb_re, precision=hi)
            - jnp.einsum('kgap,gph->gkah', cp_im, bb_im, precision=hi))
    lag = jnp.arange(L)[None, :] - jnp.arange(L)[:, None]
    tz = jnp.where((lag >= 0)[None, :, :, None, None], kern[:, jnp.clip(lag, 0, L)], 0.0)
    tz = tz.transpose(0, 1, 4, 2, 3).reshape(g, L * hch, L * hch)

    rp_re, rp_im = pw_re[L - 1 - jnp.arange(L)], pw_im[L - 1 - jnp.arange(L)]
    min_re = rp_re[:, :, :, None] * bb_re[None] - rp_im[:, :, :, None] * bb_im[None]
    min_im = rp_re[:, :, :, None] * bb_im[None] + rp_im[:, :, :, None] * bb_re[None]
    to_rows = lambda a: a.transpose(1, 0, 3, 2).reshape(g, L * hch, p)
    min_re, min_im = to_rows(min_re), to_rows(min_im)
    to_cols = lambda a: a.transpose(1, 3, 0, 2).reshape(g, p, L * hch)
    mout_re, mout_im = to_cols(cp_re[1:]), to_cols(-cp_im[1:])

    even_group = (jnp.arange(g) % 2 == 0)[:, None, None]

    def side_by_side(a, axis):
        z = jnp.zeros_like(a)
        return jnp.where(even_group, jnp.concatenate([a, z], axis=axis), jnp.concatenate([z, a], axis=axis))

    st_re, st_im = power(L * (2 ** jnp.arange(S5_SCAN_STEPS)))
    pair_lanes = lambda a: a.reshape(S5_SCAN_STEPS, g // 2, 2 * p).transpose(1, 0, 2)
    return dict(
        s5_tz=tz.astype(BF16),
        s5_min_re=side_by_side(min_re, 2).astype(BF16), s5_min_im=side_by_side(min_im, 2).astype(BF16),
        s5_mout_re=side_by_side(mout_re, 1).astype(BF16), s5_mout_im=side_by_side(mout_im, 1).astype(BF16),
        s5_pw_re=pair_lanes(st_re), s5_pw_im=pair_lanes(st_im),
        s5_d=jnp.tile(d, (1, L)).reshape(g, 1, L * hch),
    )


def _stacked_weights(p):
    depth = p['w_in'].shape[0]
    w_in = p['w_in'].astype(BF16)
    zeros = lambda n: jnp.zeros((depth, D_MODEL, n), BF16)
    kr = w_in[:, :, IN_KR:IN_B]
    rope_pad = LANES - MLA_NOPE - MLA_ROPE
    w_mix = jnp.concatenate([
        w_in[:, :, 0:IN_KR],
        zeros(MLA_NOPE), kr, zeros(rope_pad),
        zeros(MLA_NOPE), _rot_half(kr), zeros(rope_pad),
        w_in[:, :, IN_B:IN_C],
        w_in[:, :, IN_C:IN_CG], w_in[:, :, IN_CG:IN_D], zeros(LANES - 2 * N_HEADS),
        w_in[:, :, IN_D:IN_GATE]], axis=2)
    assert w_mix.shape[2] == W_MIX

    uq = p['mla_w_uq'].astype(BF16).reshape(depth, MLA_Q_RANK, N_HEADS, MLA_NOPE + MLA_ROPE)
    zq = jnp.zeros((depth, MLA_Q_RANK, N_HEADS, rope_pad), BF16)
    wq = jnp.concatenate([uq, zq], axis=-1).reshape(depth, MLA_Q_RANK, N_HEADS * LANES)
    wqr = jnp.concatenate([jnp.zeros((depth, MLA_Q_RANK, N_HEADS, MLA_NOPE), BF16),
                           _rot_half(uq[..., MLA_NOPE:]), zq], axis=-1).reshape(depth, MLA_Q_RANK, N_HEADS * LANES)
    ukv = p['mla_w_ukv'].astype(BF16).reshape(depth, MLA_KV_RANK, N_HEADS, MLA_NOPE + HEAD_W)
    zk = jnp.zeros((depth, MLA_KV_RANK, N_HEADS, LANES - MLA_NOPE), BF16)
    wk = jnp.concatenate([ukv[..., :MLA_NOPE], zk], axis=-1).reshape(depth, MLA_KV_RANK, N_HEADS * LANES)
    wv_t = jnp.concatenate([ukv[..., MLA_NOPE:], zk], axis=-1).reshape(depth, MLA_KV_RANK, N_HEADS * LANES)
    wv_t = wv_t.swapaxes(1, 2)

    spread = lambda b: jnp.repeat(b, HEAD_W, axis=-1)[:, None, :]
    row = lambda a: a[:, None, :]
    sw = dict(
        g_mix_pre=row(p['norm_mix_pre']), g_mix_post=row(p['norm_mix_post']),
        w_mix=w_mix, w_gate=w_in[:, :, IN_GATE:],
        mla_q_norm=row(p['mla_q_norm']), mla_kv_norm=row(p['mla_kv_norm']),
        wq=wq, wqr=wqr, wk=wk, wv_t=wv_t,
        hg_out_norm=row(p['hg_out_norm']),
        ml_conv_w=p['ml_conv_w'], ml_conv_b=row(p['ml_conv_b']),
        ml_wq_bd=_block_diag(p['ml_w_q'].astype(BF16)), ml_wk_bd=_block_diag(p['ml_w_k'].astype(BF16)),
        ml_i_bias=spread(p['ml_i_bias']), ml_f_bias=spread(p['ml_f_bias']),
        ml_out_norm=row(p['ml_out_norm']),
        s5_w_glu=p['s5_w_glu'].astype(BF16), s5_b_glu=row(p['s5_b_glu']),
        w_branch=p['w_branch'].astype(BF16), w_out=p['w_out'].astype(BF16),
        g_xa_pre=row(p['norm_xa_pre']), g_xa_post=row(p['norm_xa_post']), g_mem=row(p['norm_mem']),
        xa_wq=p['xa_wq'].astype(BF16), xa_wk=p['xa_wk'].astype(BF16),
        xa_wv=p['xa_wv'].astype(BF16), xa_wo=p['xa_wo'].astype(BF16),
        g_ffn_pre=row(p['norm_ffn_pre']), g_ffn_post=row(p['norm_ffn_post']),
        ffn_w_in=p['ffn_w_in'].astype(BF16), ffn_wo=p['ffn_w_out'].astype(BF16),
    )
    sw.update(jax.vmap(_s5_tables)(p['s5_a_re'], p['s5_a_im'], p['s5_log_dt'], p['s5_b_re'], p['s5_b_im'],
                                   p['s5_c_re'], p['s5_c_im'], p['s5_d']))
    return sw


def _shared_constants():
    half = MLA_ROPE // 2
    inv_freq = ROPE_THETA ** (-np.arange(half, dtype=np.float32) / half)
    freq = np.zeros((1, LANES), np.float32)
    freq[0, MLA_NOPE:MLA_NOPE + half] = inv_freq
    freq[0, MLA_NOPE + half:MLA_NOPE + MLA_ROPE] = inv_freq
    v_ones = np.zeros((N_HEADS * LANES, 1), np.float32)
    v_ones[HEAD_W::LANES] = 1.0
    return dict(freq=jnp.asarray(freq), v_ones=jnp.asarray(v_ones))


def kernel(x, mem, positions, norm_mix_pre, norm_mix_post, w_in, mla_q_norm, mla_w_uq, mla_kv_norm, mla_w_ukv, hg_lb_logits, hg_out_norm, ml_conv_w, ml_conv_b, ml_w_q, ml_w_k, ml_i_bias, ml_f_bias, ml_out_norm, s5_a_re, s5_a_im, s5_log_dt, s5_b_re, s5_b_im, s5_c_re, s5_c_im, s5_d, s5_w_glu, s5_b_glu, w_branch, w_out, norm_xa_pre, norm_xa_post, norm_mem, xa_wq, xa_wk, xa_wv, xa_wo, norm_ffn_pre, norm_ffn_post, ffn_w_in, ffn_w_out):
    p = dict(locals())
    batch, seq, _ = x.shape
    depth = w_in.shape[0]
    n = batch * seq
    tm = min(ROW_TILE, seq)
    x2 = x.reshape(n, D_MODEL)
    pos2 = positions.reshape(n, 1).astype(jnp.int32)
    stacked = _stacked_weights(p)
    shared = _shared_constants()
    for l in range(depth):
        lw = {name: _LayerParam(a, l) for name, a in stacked.items()}
        lw.update(shared)
        q, k, vt, pb, pc, pd_lo, pd_hi = _front(x2, pos2, lw, tm)
        out_a = _mla(q, k, vt, batch, seq)
        out_b = _hgrn(pb, hg_lb_logits, lw['hg_out_norm'], l, batch, seq, tm)
        out_c = _mlstm(pc, lw, batch, seq, tm)
        y_lo, y_hi = _s5(pd_lo, pd_hi, lw, batch, seq)
        x2 = _merge(x2, out_a, out_b, out_c, y_lo, y_hi, lw, tm)
        mem_k, mem_v = _memkv(mem, lw)
        x2 = _xattn(x2, mem_k, mem_v, lw, batch, seq, tm)
        x2 = _ffn(x2, lw, tm)
    return x2.reshape(batch, seq, D_MODEL)
```

```python
import functools
import math
from typing import NamedTuple

import numpy as np
import jax
import jax.numpy as jnp
from jax import lax
from jax.experimental import pallas as pl
from jax.experimental.pallas import tpu as pltpu

F32 = jnp.float32
BF16 = jnp.bfloat16

D_MODEL = 1024
NORM_EPS = 1e-6
MASK_VALUE = -1e30
GATE_FLOOR = 1e-30
BRANCH_W = 256
N_HEADS = 4
HEAD_W = 64
MLA_Q_RANK = 256
MLA_KV_RANK = 128
MLA_NOPE = 64
MLA_ROPE = 32
ROPE_THETA = 10000.0
ML_CONV = 4
S5_GROUPS = 16
S5_GROUP_CH = 16
S5_STATE = 64
S5_CHUNK = 16
S5_SCAN_STEPS = 8
XA_HEADS = 4
XA_DIM = 256
D_FF = 2816
FF_CHUNK = 256

LANES = 128
ATT_BLK = 256
ATT_SEQS = 4
CHUNK = 64
HG_CHUNK = 128
SUB = 16
ROW_TILE = 1024
SEQ_TILE = 512
VMEM_LIMIT = 56 * 1024 * 1024

OFF_AQ, OFF_AKV, OFF_KR, OFF_KRROT, OFF_MLGATE, W_SEC_A = 0, 256, 384, 512, 640, 768
W_SEC_B = 4 * BRANCH_W
W_SEC_C_MAIN = 3 * BRANCH_W
W_SEC_C = W_SEC_C_MAIN + LANES
W_SEC_D = BRANCH_W
W_MIX = W_SEC_A + W_SEC_B + W_SEC_C_MAIN + W_SEC_D
IN_KR, IN_B, IN_C, IN_CG, IN_D, IN_GATE = 384, 416, 1440, 2208, 2216, 2472


def _nt_dot(a, b):
    return lax.dot_general(a, b, (((1,), (1,)), ((), ())), preferred_element_type=F32)


def _tn_dot(a, b):
    return lax.dot_general(a, b, (((0,), (0,)), ((), ())), preferred_element_type=F32)


def _dot(a, b):
    return jnp.dot(a, b, preferred_element_type=F32)


def _dot_split(a, b_bf16, terms=3):
    acc = None
    rem = a
    for _ in range(terms):
        piece = rem.astype(BF16)
        part = _dot(piece, b_bf16)
        acc = part if acc is None else acc + part
        rem = rem - piece.astype(F32)
    return acc


def _cumsum_rows(x, tri):
    acc = None
    rem = x
    for _ in range(2):
        piece = rem.astype(BF16)
        part = _dot(tri, piece)
        acc = part if acc is None else acc + part
        rem = rem - piece.astype(F32)
    return acc


def _rms(x, gain):
    return x * lax.rsqrt(jnp.mean(x * x, axis=-1, keepdims=True) + NORM_EPS) * gain


def _sigmoid(x):
    return 0.5 + 0.5 * jnp.tanh(0.5 * x)


def _silu(x):
    return x * _sigmoid(x)


def _log_sigmoid(x):
    return jnp.minimum(x, 0.0) - jnp.log(1.0 + jnp.exp(-jnp.abs(x)))


def _gelu_tanh(x):
    c = math.sqrt(2.0 / math.pi)
    return 0.5 * x * (1.0 + jnp.tanh(c * (x + 0.044715 * (x * x * x))))


def _head_block_ones(dtype):
    r = lax.broadcasted_iota(jnp.int32, (BRANCH_W, BRANCH_W), 0) // HEAD_W
    c = lax.broadcasted_iota(jnp.int32, (BRANCH_W, BRANCH_W), 1) // HEAD_W
    return (r == c).astype(dtype)


def _head_stack_mask(m, dtype):
    r = lax.broadcasted_iota(jnp.int32, (N_HEADS * m, BRANCH_W), 0) // m
    c = lax.broadcasted_iota(jnp.int32, (N_HEADS * m, BRANCH_W), 1) // HEAD_W
    return (r == c).astype(dtype)


def _head_norm(o, gain, ones_bd):
    msq = _dot_split(o * o, ones_bd, terms=2) * (1.0 / HEAD_W)
    return o * lax.rsqrt(msq + NORM_EPS) * gain


def _chunk_lower_tri(n, chunk, dtype):
    r = lax.broadcasted_iota(jnp.int32, (n, n), 0)
    c = lax.broadcasted_iota(jnp.int32, (n, n), 1)
    return ((c <= r) & (r // chunk == c // chunk)).astype(dtype)


class _LayerParam(NamedTuple):
    array: jax.Array
    layer: int


def _const_spec(shape):
    zeros = (0,) * len(shape)
    return pl.BlockSpec(shape, lambda *_: zeros, pipeline_mode=pl.Buffered(1))


def _param_specs(consts):
    specs, operands = [], []
    for c in consts:
        if isinstance(c, _LayerParam):
            tail = c.array.shape[1:]
            idx = (c.layer,) + (0,) * len(tail)
            specs.append(pl.BlockSpec((None,) + tail, lambda *_, idx=idx: idx, pipeline_mode=pl.Buffered(1)))
            operands.append(c.array)
        else:
            specs.append(_const_spec(c.shape))
            operands.append(c)
    return specs, operands


def _front_kernel(x_ref, pos_ref, g_ref, wmix_ref, qn_ref, kvn_ref, wq_ref, wqr_ref, wk_ref, wvt_ref,
                  vones_ref, freq_ref, q_out, k_out, v_out, pb_out, pc_out, pd_lo_out, pd_hi_out):
    hn = _rms(x_ref[...], g_ref[...]).astype(BF16)
    pa = _dot(hn, wmix_ref[:, 0:W_SEC_A])
    pb_out[...] = _dot(hn, wmix_ref[:, W_SEC_A:W_SEC_A + W_SEC_B])
    pc_out[:, 0:W_SEC_C_MAIN] = _dot(hn, wmix_ref[:, W_SEC_A + W_SEC_B:W_SEC_A + W_SEC_B + W_SEC_C_MAIN])
    pc_out[:, W_SEC_C_MAIN:W_SEC_C] = pa[:, OFF_MLGATE:OFF_MLGATE + LANES]
    pd = _dot(hn, wmix_ref[:, W_SEC_A + W_SEC_B + W_SEC_C_MAIN:W_MIX])
    pd_lo_out[...] = pd[:, :LANES]
    pd_hi_out[...] = pd[:, LANES:]

    ang = pos_ref[...].astype(F32) * freq_ref[...]
    cos, sin = jnp.cos(ang), jnp.sin(ang)
    scale = (MLA_NOPE + MLA_ROPE) ** -0.5

    aqn = _rms(pa[:, OFF_AQ:OFF_AQ + MLA_Q_RANK], qn_ref[...]).astype(BF16)
    q0 = _dot(aqn, wq_ref[...])
    qr = _dot(aqn, wqr_ref[...])
    akvn = _rms(pa[:, OFF_AKV:OFF_AKV + MLA_KV_RANK], kvn_ref[...]).astype(BF16)
    kn = _dot(akvn, wk_ref[...])
    vt = _nt_dot(wvt_ref[...], akvn) + vones_ref[...]
    for t in range(v_out.shape[0]):
        v_out[t] = vt[:, t * ATT_BLK:(t + 1) * ATT_BLK].astype(v_out.dtype)
    k_rope = pa[:, OFF_KR:OFF_KR + LANES] * cos + pa[:, OFF_KRROT:OFF_KRROT + LANES] * sin
    for h in range(N_HEADS):
        sl = slice(h * LANES, (h + 1) * LANES)
        q_out[:, sl] = ((q0[:, sl] * cos + qr[:, sl] * sin) * scale).astype(q_out.dtype)
        k_out[:, sl] = (kn[:, sl] + k_rope).astype(k_out.dtype)


def _front(x2, pos2, lw, tm):
    n = x2.shape[0]
    row = lambda w: pl.BlockSpec((tm, w), lambda i: (i, 0))
    specs, consts = _param_specs([lw['g_mix_pre'], lw['w_mix'], lw['mla_q_norm'], lw['mla_kv_norm'], lw['wq'],
                                  lw['wqr'], lw['wk'], lw['wv_t'], lw['v_ones'], lw['freq']])
    return pl.pallas_call(
        _front_kernel,
        grid=(n // tm,),
        in_specs=[row(D_MODEL), row(1)] + specs,
        out_specs=[row(4 * LANES), row(4 * LANES),
                   pl.BlockSpec((tm // ATT_BLK, 4 * LANES, ATT_BLK), lambda i: (i, 0, 0)), row(W_SEC_B),
                   row(W_SEC_C), row(LANES), row(LANES)],
        out_shape=[jax.ShapeDtypeStruct((n, 4 * LANES), BF16), jax.ShapeDtypeStruct((n, 4 * LANES), BF16),
                   jax.ShapeDtypeStruct((n // ATT_BLK, 4 * LANES, ATT_BLK), BF16),
                   jax.ShapeDtypeStruct((n, W_SEC_B), F32),
                   jax.ShapeDtypeStruct((n, W_SEC_C), F32), jax.ShapeDtypeStruct((n, LANES), F32),
                   jax.ShapeDtypeStruct((n, LANES), F32)],
        compiler_params=pltpu.CompilerParams(dimension_semantics=("parallel",), vmem_limit_bytes=VMEM_LIMIT),
        name="front",
    )(x2, pos2, *consts)


def _mla_kernel(q_ref, k_ref, vt_ref, o_ref, *, blk):
    i = pl.program_id(1)
    key = lax.broadcasted_iota(jnp.int32, (blk, blk), 0)
    qry = lax.broadcasted_iota(jnp.int32, (blk, blk), 1)
    causal = key <= qry

    chains = [(s, slice(h * LANES, (h + 1) * LANES)) for s in range(q_ref.shape[0]) for h in range(N_HEADS)]

    def step(j, carry, masked):
        start = pl.multiple_of(j * blk, blk)
        scores = [_nt_dot(k_ref[s, pl.ds(start, blk), sl], q_ref[s, :, sl]) for s, sl in chains]
        if masked:
            scores = [jnp.where(causal, sc, MASK_VALUE) for sc in scores]
        m_new = [jnp.maximum(c[0], jnp.max(sc, axis=0, keepdims=True)) for c, sc in zip(carry, scores)]
        probs = [jnp.exp(sc - m).astype(BF16) for sc, m in zip(scores, m_new)]
        pv = [_dot(vt_ref[s, j, sl, :], p) for (s, sl), p in zip(chains, probs)]
        return tuple((m, jnp.exp(c[0] - m) * c[1] + x) for c, m, x in zip(carry, m_new, pv))

    init = tuple((jnp.full((1, blk), MASK_VALUE, F32), jnp.zeros((LANES, blk), F32)) for _ in chains)
    carry = lax.fori_loop(0, i, functools.partial(step, masked=False), init)
    carry = step(i, carry, True)
    outs = [acc[:HEAD_W] * (1.0 / acc[HEAD_W:HEAD_W + 1]) for _, acc in carry]
    for s in range(q_ref.shape[0]):
        o_ref[s] = jnp.concatenate(outs[s * N_HEADS:(s + 1) * N_HEADS], axis=0).T.astype(o_ref.dtype)


def _mla(q, k, vt, batch, seq):
    blk = ATT_BLK
    q3, k3 = (a.reshape(batch, seq, 4 * LANES) for a in (q, k))
    vt4 = vt.reshape(batch, seq // blk, 4 * LANES, blk)
    nseq = ATT_SEQS if batch % ATT_SEQS == 0 else 1
    out = pl.pallas_call(
        functools.partial(_mla_kernel, blk=blk),
        grid=(batch // nseq, seq // blk),
        in_specs=[pl.BlockSpec((nseq, blk, 4 * LANES), lambda b, i: (b, i, 0)),
                  pl.BlockSpec((nseq, seq, 4 * LANES), lambda b, i: (b, 0, 0)),
                  pl.BlockSpec((nseq, seq // blk, 4 * LANES, blk), lambda b, i: (b, 0, 0, 0))],
        out_specs=pl.BlockSpec((nseq, blk, BRANCH_W), lambda b, i: (b, i, 0)),
        out_shape=jax.ShapeDtypeStruct((batch, seq, BRANCH_W), BF16),
        compiler_params=pltpu.CompilerParams(dimension_semantics=("parallel", "arbitrary"),
                                             vmem_limit_bytes=VMEM_LIMIT),
        name="mla",
    )(q3, k3, vt4)
    return out.reshape(batch * seq, BRANCH_W)


def _hgrn_kernel(pb_ref, lbl_ref, gain_ref, o_ref, st_ref, q_sc, k_sc, b_sc, lk_sc, o_sc, *, layer, tb):
    @pl.when(pl.program_id(1) == 0)
    def _():
        st_ref[...] = jnp.zeros_like(st_ref)

    lg = lbl_ref[...]
    e = jnp.exp(lg - jnp.max(lg, axis=0, keepdims=True))
    sm = e / jnp.sum(e, axis=0, keepdims=True)
    lb = jnp.zeros((1, BRANCH_W), F32)
    for r in range(1, layer + 1):
        lb = lb + sm[r:r + 1, :]
    one_m_lb = 1.0 - lb

    ones_bd = _head_block_ones(BF16)
    ones_bd_f32 = _head_block_ones(F32)
    row_in_sub = lax.broadcasted_iota(jnp.int32, (HG_CHUNK, BRANCH_W), 0) % SUB

    f_logit = pb_ref[:, BRANCH_W:2 * BRANCH_W]
    e = jnp.exp(-jnp.abs(f_logit))
    r = 1.0 / (1.0 + e)
    nonneg = f_logit >= 0.0
    forget = lb + one_m_lb * jnp.where(nonneg, r, e * r)
    b_all = _cumsum_rows(jnp.log(jnp.maximum(forget, GATE_FLOOR)), _chunk_lower_tri(tb, HG_CHUNK, BF16))
    b_sc[...] = b_all
    k_sc[...] = one_m_lb * jnp.where(nonneg, e * r, r)
    lk_sc[...] = jnp.log(one_m_lb) - jnp.maximum(f_logit, 0.0) + jnp.log(r) - b_all
    q_sc[...] = _silu(pb_ref[:, 0:BRANCH_W])

    def chunk_body(c, carry):
        start = pl.multiple_of(c * HG_CHUNK, HG_CHUNK)
        rows = pl.ds(start, HG_CHUNK)
        q, k, b = q_sc[rows, :], k_sc[rows, :], b_sc[rows, :]
        v = pb_ref[rows, 2 * BRANCH_W:3 * BRANCH_W]
        b_end = b[HG_CHUNK - 1:HG_CHUNK, :]

        st = st_ref[...]
        o = _nt_dot((q * jnp.exp(b)).astype(BF16), st.astype(BF16))
        k_end = k * jnp.exp(b_end - b)
        st_ref[...] = (st * jnp.exp(b_end) + _tn_dot(v.astype(BF16), k_end.astype(BF16))) * ones_bd_f32

        m = SUB
        while m < HG_CHUNK:
            mask = _head_stack_mask(m, F32)
            parts = []
            for lo in range(0, HG_CHUNK, 2 * m):
                left = slice(lo, lo + m)
                right = slice(lo + m, lo + 2 * m)
                ref = b[lo + m - 1:lo + m, :]
                kt = k[left] * jnp.exp(ref - b[left])
                qt = q[right] * jnp.exp(b[right] - ref)
                k_hat = (jnp.concatenate([kt] * N_HEADS, axis=0) * mask).astype(BF16)
                v_hat = (jnp.concatenate([v[left]] * N_HEADS, axis=0) * mask).astype(BF16)
                a = _nt_dot(qt.astype(BF16), k_hat)
                parts.append(jnp.zeros((m, BRANCH_W), F32))
                parts.append(_dot(a.astype(BF16), v_hat))
            o = o + jnp.concatenate(parts, axis=0)
            m *= 2

        def key_row(ref, col, s):
            return jnp.concatenate(
                [jnp.broadcast_to(ref[pl.ds(start + blk * SUB + s, 1), col:col + BRANCH_W], (SUB, BRANCH_W))
                 for blk in range(HG_CHUNK // SUB)], axis=0)

        for s in range(SUB):
            expo = b + key_row(lk_sc, 0, s)
            if s > 0:
                expo = jnp.where(row_in_sub >= s, expo, MASK_VALUE)
            score = _dot((q * jnp.exp(expo)).astype(BF16), ones_bd)
            o = o + score * key_row(pb_ref, 2 * BRANCH_W, s)
        o_sc[rows, :] = o
        return carry

    lax.fori_loop(0, tb // HG_CHUNK, chunk_body, 0, unroll=2)

    gate = _silu(pb_ref[:, 3 * BRANCH_W:4 * BRANCH_W])
    o_ref[...] = (_head_norm(o_sc[...], gain_ref[...], ones_bd) * gate).astype(o_ref.dtype)


def _hgrn(pb, lb_logits, gain, layer, batch, seq, tb):
    pb3 = pb.reshape(batch, seq, W_SEC_B)
    specs, consts = _param_specs([lb_logits, gain])
    blk_scratch = pltpu.VMEM((tb, BRANCH_W), F32)
    out = pl.pallas_call(
        functools.partial(_hgrn_kernel, layer=layer, tb=tb),
        grid=(batch, seq // tb),
        in_specs=[pl.BlockSpec((None, tb, W_SEC_B), lambda b, i: (b, i, 0))] + specs,
        out_specs=pl.BlockSpec((None, tb, BRANCH_W), lambda b, i: (b, i, 0)),
        out_shape=jax.ShapeDtypeStruct((batch, seq, BRANCH_W), BF16),
        scratch_shapes=[pltpu.VMEM((BRANCH_W, BRANCH_W), F32)] + [blk_scratch] * 5,
        compiler_params=pltpu.CompilerParams(dimension_semantics=("parallel", "arbitrary"),
                                             vmem_limit_bytes=VMEM_LIMIT),
        name="hgrn",
    )(pb3, *consts)
    return out.reshape(batch * seq, BRANCH_W)


def _mlstm_kernel(pc_ref, cw_ref, cb_ref, wq_ref, wk_ref, ib_ref, fb_ref, gain_ref, o_ref,
                  xext_ref, q_sc, k_sc, il_sc, b_sc, h_sc, c_ref, n_ref, m_ref, *, tb):
    pad = 8

    @pl.when(pl.program_id(1) == 0)
    def _():
        xext_ref[0:pad, :] = jnp.zeros((pad, BRANCH_W), F32)
        c_ref[...] = jnp.zeros_like(c_ref)
        n_ref[...] = jnp.zeros_like(n_ref)
        m_ref[...] = jnp.zeros_like(m_ref)

    xext_ref[pad:pad + tb, :] = pc_ref[:, 0:BRANCH_W]
    conv = jnp.zeros((tb, BRANCH_W), F32) + cb_ref[...]
    for j in range(ML_CONV):
        conv = conv + xext_ref[pad - (ML_CONV - 1) + j:pad - (ML_CONV - 1) + j + tb, :] * cw_ref[j:j + 1, :]
    xext_ref[0:pad, :] = xext_ref[tb:tb + pad, :]
    xc = _silu(conv).astype(BF16)
    q_sc[...] = _dot(xc, wq_ref[...])
    k_sc[...] = _dot(xc, wk_ref[...]) * (HEAD_W ** -0.5)

    gr = lax.broadcasted_iota(jnp.int32, (LANES, BRANCH_W), 0)
    gc = lax.broadcasted_iota(jnp.int32, (LANES, BRANCH_W), 1) // HEAD_W
    gates = pc_ref[:, 3 * BRANCH_W:3 * BRANCH_W + LANES]
    il_sc[...] = _dot_split(gates, (gr == gc).astype(BF16)) + ib_ref[...]
    f_log = _log_sigmoid(_dot_split(gates, (gr == gc + N_HEADS).astype(BF16)) + fb_ref[...])
    b_sc[...] = _cumsum_rows(f_log, _chunk_lower_tri(tb, CHUNK, BF16))

    ones_bd = _head_block_ones(BF16)
    ones_bd_f32 = _head_block_ones(F32)
    stack_mask = _head_stack_mask(CHUNK, F32)
    lane = lax.broadcasted_iota(jnp.int32, (CHUNK, BRANCH_W), 1)
    rowi = lax.broadcasted_iota(jnp.int32, (CHUNK, BRANCH_W), 0)
    diag_sel = (lane % HEAD_W == rowi).astype(F32)
    causal = (lane % HEAD_W) <= rowi

    def head_max(a):
        out = jnp.zeros_like(a)
        for h in range(N_HEADS):
            mx = jnp.max(a[:, h * HEAD_W:(h + 1) * HEAD_W], axis=-1, keepdims=True)
            out = jnp.where(lane // HEAD_W == h, mx, out)
        return out

    def chunk_body(c, carry):
        rows = pl.ds(pl.multiple_of(c * CHUNK, CHUNK), CHUNK)
        q, k, i_log, b = q_sc[rows, :], k_sc[rows, :], il_sc[rows, :], b_sc[rows, :]
        v = pc_ref[rows, BRANCH_W:2 * BRANCH_W]
        qb = q.astype(BF16)
        k_hat = (jnp.concatenate([k] * N_HEADS, axis=0) * stack_mask).astype(BF16)
        v_hat = (jnp.concatenate([v] * N_HEADS, axis=0) * stack_mask).astype(BF16)
        scores = _nt_dot(qb, k_hat)
        c0, n0, m0 = c_ref[...], n_ref[...], m_ref[...]
        inter_num = _dot(qb, c0.astype(BF16))
        inter_den = _dot((q * n0).astype(BF16), ones_bd)

        b_end = b[CHUNK - 1:CHUNK, :]
        b_row = jnp.sum(b * diag_sel, axis=0, keepdims=True)
        i_row = jnp.sum(i_log * diag_sel, axis=0, keepdims=True)
        dmat = jnp.where(causal, b - b_row + i_row, MASK_VALUE)
        a = b + m0
        m_t = jnp.maximum(a, head_max(dmat))
        w_inter = jnp.exp(a - m_t)
        qk = (scores * jnp.exp(dmat - m_t)).astype(BF16)
        num = _dot(qk, v_hat) + w_inter * inter_num
        den = _dot(qk, ones_bd) + w_inter * inter_den
        h_sc[rows, :] = num / jnp.maximum(jnp.abs(den), jnp.exp(-m_t))

        g_end = b_end - b + i_log
        m_new = jnp.maximum(b_end + m0, jnp.max(g_end, axis=0, keepdims=True))
        keep = jnp.exp(b_end + m0 - m_new)
        kw = k * jnp.exp(g_end - m_new)
        c_ref[...] = (keep * c0 + _tn_dot(kw.astype(BF16), v.astype(BF16))) * ones_bd_f32
        n_ref[...] = keep * n0 + jnp.sum(kw, axis=0, keepdims=True)
        m_ref[...] = m_new
        return carry

    lax.fori_loop(0, tb // CHUNK, chunk_body, 0, unroll=2)

    out_gate = _sigmoid(pc_ref[:, 2 * BRANCH_W:3 * BRANCH_W])
    o_ref[...] = (out_gate * _head_norm(h_sc[...], gain_ref[...], ones_bd)).astype(o_ref.dtype)


def _mlstm(pc, lw, batch, seq, tb):
    pc3 = pc.reshape(batch, seq, W_SEC_C)
    specs, consts = _param_specs([lw['ml_conv_w'], lw['ml_conv_b'], lw['ml_wq_bd'], lw['ml_wk_bd'],
                                  lw['ml_i_bias'], lw['ml_f_bias'], lw['ml_out_norm']])
    blk_scratch = pltpu.VMEM((tb, BRANCH_W), F32)
    out = pl.pallas_call(
        functools.partial(_mlstm_kernel, tb=tb),
        grid=(batch, seq // tb),
        in_specs=[pl.BlockSpec((None, tb, W_SEC_C), lambda b, i: (b, i, 0))] + specs,
        out_specs=pl.BlockSpec((None, tb, BRANCH_W), lambda b, i: (b, i, 0)),
        out_shape=jax.ShapeDtypeStruct((batch, seq, BRANCH_W), BF16),
        scratch_shapes=[pltpu.VMEM((tb + 8, BRANCH_W), F32)] + [blk_scratch] * 5
                       + [pltpu.VMEM((BRANCH_W, BRANCH_W), F32), pltpu.VMEM((1, BRANCH_W), F32),
                          pltpu.VMEM((1, BRANCH_W), F32)],
        compiler_params=pltpu.CompilerParams(dimension_semantics=("parallel", "arbitrary"),
                                             vmem_limit_bytes=VMEM_LIMIT),
        name="mlstm",
    )(pc3, *consts)
    return out.reshape(batch * seq, BRANCH_W)


def _s5_kernel(u_lo_ref, u_hi_ref, tz_ref, min_re_ref, min_im_ref, mout_re_ref, mout_im_ref, pw_re_ref,
               pw_im_ref, d_ref, y_lo_ref, y_hi_ref, ug_ref, yl_ref, *, n_rows):
    gw = S5_GROUP_CH
    half = S5_GROUPS // 2
    for l in range(S5_CHUNK):
        for part, u_ref in enumerate((u_lo_ref, u_hi_ref)):
            x_l = u_ref[pl.ds(l, n_rows, stride=S5_CHUNK), :]
            for g in range(half):
                ug_ref[part * half + g, :, l * gw:(l + 1) * gw] = x_l[:, g * gw:(g + 1) * gw]

    row = lax.broadcasted_iota(jnp.int32, (n_rows, LANES), 0)
    for pair in range(S5_GROUPS // 2):
        u = [ug_ref[2 * pair + s] for s in range(2)]
        ub = [a.astype(BF16) for a in u]
        s_re = _dot(ub[0], min_re_ref[2 * pair]) + _dot(ub[1], min_re_ref[2 * pair + 1])
        s_im = _dot(ub[0], min_im_ref[2 * pair]) + _dot(ub[1], min_im_ref[2 * pair + 1])
        step, k = 1, 0
        while step < n_rows:
            keep = row >= step
            p_re = jnp.where(keep, pltpu.roll(s_re, step, 0), 0.0)
            p_im = jnp.where(keep, pltpu.roll(s_im, step, 0), 0.0)
            a_re = pw_re_ref[pair, k:k + 1, :]
            a_im = pw_im_ref[pair, k:k + 1, :]
            s_re, s_im = s_re + a_re * p_re - a_im * p_im, s_im + a_re * p_im + a_im * p_re
            step, k = step * 2, k + 1
        first = row >= 1
        s0_re = jnp.where(first, pltpu.roll(s_re, 1, 0), 0.0).astype(BF16)
        s0_im = jnp.where(first, pltpu.roll(s_im, 1, 0), 0.0).astype(BF16)
        for s in range(2):
            g = 2 * pair + s
            y = (_dot(ub[s], tz_ref[g]) + _dot(s0_re, mout_re_ref[g]) + _dot(s0_im, mout_im_ref[g])
                 + u[s] * d_ref[g])
            for l in range(S5_CHUNK):
                yl_ref[l, :, g * gw:(g + 1) * gw] = y[:, l * gw:(l + 1) * gw]

    for l in range(S5_CHUNK):
        y_lo_ref[pl.ds(l, n_rows, stride=S5_CHUNK), :] = yl_ref[l, :, :LANES]
        y_hi_ref[pl.ds(l, n_rows, stride=S5_CHUNK), :] = yl_ref[l, :, LANES:]


def _s5(pd_lo, pd_hi, lw, batch, seq):
    n_rows = seq // S5_CHUNK
    assert n_rows <= 2 ** S5_SCAN_STEPS
    width = S5_CHUNK * S5_GROUP_CH
    specs, consts = _param_specs([lw['s5_tz'], lw['s5_min_re'], lw['s5_min_im'], lw['s5_mout_re'],
                                  lw['s5_mout_im'], lw['s5_pw_re'], lw['s5_pw_im'], lw['s5_d']])
    seq_blk = pl.BlockSpec((None, seq, LANES), lambda b: (b, 0, 0))
    out = jax.ShapeDtypeStruct((batch, seq, LANES), F32)
    y_lo, y_hi = pl.pallas_call(
        functools.partial(_s5_kernel, n_rows=n_rows),
        grid=(batch,),
        in_specs=[seq_blk, seq_blk] + specs,
        out_specs=[seq_blk, seq_blk],
        out_shape=[out, out],
        scratch_shapes=[pltpu.VMEM((S5_GROUPS, n_rows, width), F32), pltpu.VMEM((S5_CHUNK, n_rows, BRANCH_W), F32)],
        compiler_params=pltpu.CompilerParams(dimension_semantics=("parallel",), vmem_limit_bytes=VMEM_LIMIT),
        name="s5",
    )(pd_lo.reshape(batch, seq, LANES), pd_hi.reshape(batch, seq, LANES), *consts)
    return y_lo.reshape(batch * seq, LANES), y_hi.reshape(batch * seq, LANES)


def _merge_kernel(x_ref, a_ref, b_ref, c_ref, y_lo_ref, y_hi_ref, gpre_ref, wgate_ref, wbr_ref, wout_ref,
                  wglu_ref, bglu_ref, gpost_ref, o_ref):
    x = x_ref[...]
    hn = _rms(x, gpre_ref[...]).astype(BF16)
    y = _gelu_tanh(jnp.concatenate([y_lo_ref[...], y_hi_ref[...]], axis=-1))
    out_d = (y * _sigmoid(_dot(y.astype(BF16), wglu_ref[...]) + bglu_ref[...])).astype(BF16)
    branches = (a_ref[...], b_ref[...], c_ref[...], out_d)
    merged = None
    for n in range(4):
        gate = _sigmoid(_dot(hn, wgate_ref[:, n * D_MODEL:(n + 1) * D_MODEL]))
        term = gate * _dot(branches[n], wbr_ref[n])
        merged = term if merged is None else merged + term
    mix = _dot(merged.astype(BF16), wout_ref[...])
    o_ref[...] = x + _rms(mix, gpost_ref[...])


def _merge(x2, out_a, out_b, out_c, y_lo, y_hi, lw, tm):
    n = x2.shape[0]
    row = lambda w: pl.BlockSpec((tm, w), lambda i: (i, 0))
    specs, consts = _param_specs([lw['g_mix_pre'], lw['w_gate'], lw['w_branch'], lw['w_out'], lw['s5_w_glu'],
                                  lw['s5_b_glu'], lw['g_mix_post']])
    return pl.pallas_call(
        _merge_kernel,
        grid=(n // tm,),
        in_specs=[row(D_MODEL), row(BRANCH_W), row(BRANCH_W), row(BRANCH_W), row(LANES), row(LANES)] + specs,
        out_specs=row(D_MODEL),
        out_shape=jax.ShapeDtypeStruct((n, D_MODEL), F32),
        compiler_params=pltpu.CompilerParams(dimension_semantics=("parallel",), vmem_limit_bytes=VMEM_LIMIT),
        name="merge",
    )(x2, out_a, out_b, out_c, y_lo, y_hi, *consts)


def _memkv_kernel(mem_ref, g_ref, wk_ref, wv_ref, k_out, v_out):
    mn = _rms(mem_ref[...], g_ref[...]).astype(BF16)
    k_out[...] = _dot(mn, wk_ref[...]).astype(k_out.dtype)
    v_out[...] = _dot(mn, wv_ref[...]).astype(v_out.dtype)


def _memkv(mem, lw):
    batch, mlen, _ = mem.shape
    specs, consts = _param_specs([lw['g_mem'], lw['xa_wk'], lw['xa_wv']])
    blk = pl.BlockSpec((None, mlen, D_MODEL), lambda b: (b, 0, 0))
    return pl.pallas_call(
        _memkv_kernel,
        grid=(batch,),
        in_specs=[blk] + specs,
        out_specs=[blk, blk],
        out_shape=[jax.ShapeDtypeStruct((batch, mlen, D_MODEL), BF16)] * 2,
        compiler_params=pltpu.CompilerParams(dimension_semantics=("parallel",), vmem_limit_bytes=VMEM_LIMIT),
        name="memkv",
    )(mem, *consts)


def _xattn_kernel(x_ref, k_ref, v_ref, gpre_ref, wq_ref, wo_ref, gpost_ref, o_ref):
    x = x_ref[...]
    hn = _rms(x, gpre_ref[...]).astype(BF16)
    q = (_dot(hn, wq_ref[...]) * (XA_DIM ** -0.5)).astype(BF16)
    outs = []
    for h in range(XA_HEADS):
        sl = slice(h * XA_DIM, (h + 1) * XA_DIM)
        s = _nt_dot(q[:, sl], k_ref[:, sl])
        p = jnp.exp(s - jnp.max(s, axis=-1, keepdims=True))
        inv = 1.0 / jnp.sum(p, axis=-1, keepdims=True)
        outs.append((_dot(p.astype(BF16), v_ref[:, sl]) * inv).astype(BF16))
    xa = _dot(jnp.concatenate(outs, axis=-1), wo_ref[...])
    o_ref[...] = x + _rms(xa, gpost_ref[...])


def _xattn(x2, mem_k, mem_v, lw, batch, seq, tm):
    mlen = mem_k.shape[1]
    x3 = x2.reshape(batch, seq, D_MODEL)
    specs, consts = _param_specs([lw['g_xa_pre'], lw['xa_wq'], lw['xa_wo'], lw['g_xa_post']])
    row = pl.BlockSpec((None, tm, D_MODEL), lambda b, i: (b, i, 0))
    kv = pl.BlockSpec((None, mlen, D_MODEL), lambda b, i: (b, 0, 0))
    out = pl.pallas_call(
        _xattn_kernel,
        grid=(batch, seq // tm),
        in_specs=[row, kv, kv] + specs,
        out_specs=row,
        out_shape=jax.ShapeDtypeStruct((batch, seq, D_MODEL), F32),
        compiler_params=pltpu.CompilerParams(dimension_semantics=("parallel", "parallel"),
                                             vmem_limit_bytes=VMEM_LIMIT),
        name="xattn",
    )(x3, mem_k, mem_v, *consts)
    return out.reshape(batch * seq, D_MODEL)


def _ffn_kernel(x_ref, gpre_ref, win_ref, wo_ref, gpost_ref, o_ref):
    x = x_ref[...]
    hn = _rms(x, gpre_ref[...]).astype(BF16)
    acc = None
    for j in range(D_FF // FF_CHUNK):
        lo = j * FF_CHUNK
        a = _dot(hn, win_ref[:, lo:lo + FF_CHUNK])
        b = _dot(hn, win_ref[:, D_FF + lo:D_FF + lo + FF_CHUNK])
        part = _dot((_silu(a) * b).astype(BF16), wo_ref[lo:lo + FF_CHUNK, :])
        acc = part if acc is None else acc + part
    o_ref[...] = x + _rms(acc, gpost_ref[...])


def _ffn(x2, lw, tm):
    n = x2.shape[0]
    row = pl.BlockSpec((tm, D_MODEL), lambda i: (i, 0))
    specs, consts = _param_specs([lw['g_ffn_pre'], lw['ffn_w_in'], lw['ffn_wo'], lw['g_ffn_post']])
    return pl.pallas_call(
        _ffn_kernel,
        grid=(n // tm,),
        in_specs=[row] + specs,
        out_specs=row,
        out_shape=jax.ShapeDtypeStruct((n, D_MODEL), F32),
        compiler_params=pltpu.CompilerParams(dimension_semantics=("parallel",), vmem_limit_bytes=VMEM_LIMIT),
        name="ffn",
    )(x2, *consts)


def _rot_half(w):
    half = MLA_ROPE // 2
    return jnp.concatenate([-w[..., half:], w[..., :half]], axis=-1)


def _block_diag(w):
    depth, h, d, e = w.shape
    eye = jnp.eye(h, dtype=w.dtype)
    return (eye[None, :, None, :, None] * w[:, :, :, None, :]).reshape(depth, h * d, h * e)


def _s5_tables(a_re, a_im, log_dt, b_re, b_im, c_re, c_im, d):
    hi = lax.Precision.HIGHEST
    g, p, hch = b_re.shape
    L = S5_CHUNK
    width = L * hch
    dt = jnp.exp(log_dt)[:, None]
    lam_re, lam_im = a_re * dt, a_im * dt
    mag = jnp.exp(lam_re)
    ab_re, ab_im = mag * jnp.cos(lam_im), mag * jnp.sin(lam_im)
    inv_abs2 = 1.0 / (a_re * a_re + a_im * a_im)
    z_re = ((ab_re - 1.0) * a_re + ab_im * a_im) * inv_abs2
    z_im = (ab_im * a_re - (ab_re - 1.0) * a_im) * inv_abs2
    bt_re, bt_im = b_re.swapaxes(1, 2), b_im.swapaxes(1, 2)
    bbt_re = z_re[:, None, :] * bt_re - z_im[:, None, :] * bt_im
    bbt_im = z_re[:, None, :] * bt_im + z_im[:, None, :] * bt_re

    def power(k):
        k = jnp.asarray(k, F32)[None, :, None]
        m = jnp.exp(k * lam_re[:, None, :])
        return m * jnp.cos(k * lam_im[:, None, :]), m * jnp.sin(k * lam_im[:, None, :])

    pw_re, pw_im = power(jnp.arange(L + 1))
    ct_re, ct_im = c_re.swapaxes(1, 2)[:, :, None, :], c_im.swapaxes(1, 2)[:, :, None, :]
    pk_re, pk_im = pw_re.swapaxes(1, 2)[:, :, :, None], pw_im.swapaxes(1, 2)[:, :, :, None]
    cp_re = (ct_re * pk_re - ct_im * pk_im).reshape(g, p, (L + 1) * hch)
    cp_im = (ct_re * pk_im + ct_im * pk_re).reshape(g, p, (L + 1) * hch)
    kern = (jnp.einsum('ghp,gpx->ghx', bbt_re, cp_re, precision=hi)
            - jnp.einsum('ghp,gpx->ghx', bbt_im, cp_im, precision=hi))
    x = jnp.arange((L + 1) * hch)
    y = jnp.arange(width)
    lag = y[None, None, :] // hch - jnp.arange(L)[:, None, None]
    place = ((x[None, :, None] // hch == lag) & (x[None, :, None] % hch == y[None, None, :] % hch)).astype(BF16)
    tz = jnp.einsum('ghx,lxy->glhy', kern.astype(BF16), place, preferred_element_type=F32)
    tz = tz.astype(BF16).reshape(g, width, width)

    rp_re, rp_im = pw_re[:, L - 1 - jnp.arange(L), None, :], pw_im[:, L - 1 - jnp.arange(L), None, :]
    min_re = (rp_re * bbt_re[:, None] - rp_im * bbt_im[:, None]).reshape(g, width, p)
    min_im = (rp_re * bbt_im[:, None] + rp_im * bbt_re[:, None]).reshape(g, width, p)
    mout_re, mout_im = cp_re[:, :, hch:], -cp_im[:, :, hch:]

    even_group = (jnp.arange(g) % 2 == 0)[:, None, None]

    def side_by_side(a, axis):
        z = jnp.zeros_like(a)
        return jnp.where(even_group, jnp.concatenate([a, z], axis=axis), jnp.concatenate([z, a], axis=axis))

    st_re, st_im = power(L * (2 ** jnp.arange(S5_SCAN_STEPS)))
    pair_lanes = lambda a: a.reshape(g // 2, 2, S5_SCAN_STEPS, p).swapaxes(1, 2).reshape(g // 2, S5_SCAN_STEPS, 2 * p)
    return dict(
        s5_tz=tz,
        s5_min_re=side_by_side(min_re, 2).astype(BF16), s5_min_im=side_by_side(min_im, 2).astype(BF16),
        s5_mout_re=side_by_side(mout_re, 1).astype(BF16), s5_mout_im=side_by_side(mout_im, 1).astype(BF16),
        s5_pw_re=pair_lanes(st_re), s5_pw_im=pair_lanes(st_im),
        s5_d=jnp.tile(d, (1, L)).reshape(g, 1, width),
    )


def _stacked_weights(p):
    depth = p['w_in'].shape[0]
    w_in = p['w_in'].astype(BF16)
    zeros = lambda n: jnp.zeros((depth, D_MODEL, n), BF16)
    kr = w_in[:, :, IN_KR:IN_B]
    rope_pad = LANES - MLA_NOPE - MLA_ROPE
    w_mix = jnp.concatenate([
        w_in[:, :, 0:IN_KR],
        zeros(MLA_NOPE), kr, zeros(rope_pad),
        zeros(MLA_NOPE), _rot_half(kr), zeros(rope_pad),
        w_in[:, :, IN_CG:IN_D], zeros(LANES - 2 * N_HEADS),
        w_in[:, :, IN_B:IN_C],
        w_in[:, :, IN_C:IN_CG],
        w_in[:, :, IN_D:IN_GATE]], axis=2)
    assert w_mix.shape[2] == W_MIX

    uq = p['mla_w_uq'].astype(BF16).reshape(depth, MLA_Q_RANK, N_HEADS, MLA_NOPE + MLA_ROPE)
    zq = jnp.zeros((depth, MLA_Q_RANK, N_HEADS, rope_pad), BF16)
    wq = jnp.concatenate([uq, zq], axis=-1).reshape(depth, MLA_Q_RANK, N_HEADS * LANES)
    wqr = jnp.concatenate([jnp.zeros((depth, MLA_Q_RANK, N_HEADS, MLA_NOPE), BF16),
                           _rot_half(uq[..., MLA_NOPE:]), zq], axis=-1).reshape(depth, MLA_Q_RANK, N_HEADS * LANES)
    ukv = p['mla_w_ukv'].astype(BF16).reshape(depth, MLA_KV_RANK, N_HEADS, MLA_NOPE + HEAD_W)
    zk = jnp.zeros((depth, MLA_KV_RANK, N_HEADS, LANES - MLA_NOPE), BF16)
    wk = jnp.concatenate([ukv[..., :MLA_NOPE], zk], axis=-1).reshape(depth, MLA_KV_RANK, N_HEADS * LANES)
    wv_t = jnp.concatenate([ukv[..., MLA_NOPE:], zk], axis=-1).reshape(depth, MLA_KV_RANK, N_HEADS * LANES)
    wv_t = wv_t.swapaxes(1, 2)

    spread = lambda b: jnp.repeat(b, HEAD_W, axis=-1)[:, None, :]
    row = lambda a: a[:, None, :]
    sw = dict(
        g_mix_pre=row(p['norm_mix_pre']), g_mix_post=row(p['norm_mix_post']),
        w_mix=w_mix, w_gate=w_in[:, :, IN_GATE:],
        mla_q_norm=row(p['mla_q_norm']), mla_kv_norm=row(p['mla_kv_norm']),
        wq=wq, wqr=wqr, wk=wk, wv_t=wv_t,
        hg_out_norm=row(p['hg_out_norm']),
        ml_conv_w=p['ml_conv_w'], ml_conv_b=row(p['ml_conv_b']),
        ml_wq_bd=_block_diag(p['ml_w_q'].astype(BF16)), ml_wk_bd=_block_diag(p['ml_w_k'].astype(BF16)),
        ml_i_bias=spread(p['ml_i_bias']), ml_f_bias=spread(p['ml_f_bias']),
        ml_out_norm=row(p['ml_out_norm']),
        s5_w_glu=p['s5_w_glu'].astype(BF16), s5_b_glu=row(p['s5_b_glu']),
        w_branch=p['w_branch'].astype(BF16), w_out=p['w_out'].astype(BF16),
        g_xa_pre=row(p['norm_xa_pre']), g_xa_post=row(p['norm_xa_post']), g_mem=row(p['norm_mem']),
        xa_wq=p['xa_wq'].astype(BF16), xa_wk=p['xa_wk'].astype(BF16),
        xa_wv=p['xa_wv'].astype(BF16), xa_wo=p['xa_wo'].astype(BF16),
        g_ffn_pre=row(p['norm_ffn_pre']), g_ffn_post=row(p['norm_ffn_post']),
        ffn_w_in=p['ffn_w_in'].astype(BF16), ffn_wo=p['ffn_w_out'].astype(BF16),
    )
    sw.update(jax.vmap(_s5_tables)(p['s5_a_re'], p['s5_a_im'], p['s5_log_dt'], p['s5_b_re'], p['s5_b_im'],
                                   p['s5_c_re'], p['s5_c_im'], p['s5_d']))
    return sw


def _shared_constants():
    half = MLA_ROPE // 2
    inv_freq = ROPE_THETA ** (-np.arange(half, dtype=np.float32) / half)
    freq = np.zeros((1, LANES), np.float32)
    freq[0, MLA_NOPE:MLA_NOPE + half] = inv_freq
    freq[0, MLA_NOPE + half:MLA_NOPE + MLA_ROPE] = inv_freq
    v_ones = np.zeros((N_HEADS * LANES, 1), np.float32)
    v_ones[HEAD_W::LANES] = 1.0
    return dict(freq=jnp.asarray(freq), v_ones=jnp.asarray(v_ones))


def kernel(x, mem, positions, norm_mix_pre, norm_mix_post, w_in, mla_q_norm, mla_w_uq, mla_kv_norm, mla_w_ukv, hg_lb_logits, hg_out_norm, ml_conv_w, ml_conv_b, ml_w_q, ml_w_k, ml_i_bias, ml_f_bias, ml_out_norm, s5_a_re, s5_a_im, s5_log_dt, s5_b_re, s5_b_im, s5_c_re, s5_c_im, s5_d, s5_w_glu, s5_b_glu, w_branch, w_out, norm_xa_pre, norm_xa_post, norm_mem, xa_wq, xa_wk, xa_wv, xa_wo, norm_ffn_pre, norm_ffn_post, ffn_w_in, ffn_w_out):
    p = dict(locals())
    batch, seq, _ = x.shape
    depth = w_in.shape[0]
    n = batch * seq
    tm = min(ROW_TILE, seq)
    tb = min(SEQ_TILE, seq)
    x2 = x.reshape(n, D_MODEL)
    pos2 = positions.reshape(n, 1).astype(jnp.int32)
    stacked = _stacked_weights(p)
    shared = _shared_constants()
    for l in range(depth):
        lw = {name: _LayerParam(a, l) for name, a in stacked.items()}
        lw.update(shared)
        q, k, vt, pb, pc, pd_lo, pd_hi = _front(x2, pos2, lw, tm)
        out_a = _mla(q, k, vt, batch, seq)
        out_b = _hgrn(pb, hg_lb_logits, lw['hg_out_norm'], l, batch, seq, tb)
        out_c = _mlstm(pc, lw, batch, seq, tb)
        y_lo, y_hi = _s5(pd_lo, pd_hi, lw, batch, seq)
        x2 = _merge(x2, out_a, out_b, out_c, y_lo, y_hi, lw, tm)
        mem_k, mem_v = _memkv(mem, lw)
        x2 = _xattn(x2, mem_k, mem_v, lw, batch, seq, tm)
        x2 = _ffn(x2, lw, tm)
    return x2.reshape(batch, seq, D_MODEL)
```

```python
import functools
import math
from typing import NamedTuple

import numpy as np
import jax
import jax.numpy as jnp
from jax import lax
from jax.experimental import pallas as pl
from jax.experimental.pallas import tpu as pltpu

F32 = jnp.float32
BF16 = jnp.bfloat16

D_MODEL = 1024
NORM_EPS = 1e-6
MASK_VALUE = -1e30
GATE_FLOOR = 1e-30
BRANCH_W = 256
N_HEADS = 4
HEAD_W = 64
MLA_Q_RANK = 256
MLA_KV_RANK = 128
MLA_NOPE = 64
MLA_ROPE = 32
ROPE_THETA = 10000.0
ML_CONV = 4
S5_GROUPS = 16
S5_GROUP_CH = 16
S5_STATE = 64
S5_CHUNK = 16
S5_SCAN_STEPS = 8
XA_HEADS = 4
XA_DIM = 256
D_FF = 2816
FF_CHUNK = 256

LANES = 128
ATT_BLK = 256
MIX_SEQS = 2
ATT_SEQS = 4
CHUNK = 64
HG_CHUNK = 128
SUB = 16
ROW_TILE = 1024
SEQ_TILE = 512
VMEM_LIMIT = 56 * 1024 * 1024

OFF_AQ, OFF_AKV, OFF_KR, OFF_KRROT, OFF_MLGATE, W_SEC_A = 0, 256, 384, 512, 640, 768
W_SEC_B = 4 * BRANCH_W
W_SEC_C_MAIN = 3 * BRANCH_W
W_SEC_C = W_SEC_C_MAIN + LANES
W_SEC_D = BRANCH_W
W_MIX = W_SEC_A + W_SEC_B + W_SEC_C_MAIN + W_SEC_D
IN_KR, IN_B, IN_C, IN_CG, IN_D, IN_GATE = 384, 416, 1440, 2208, 2216, 2472


def _nt_dot(a, b):
    return lax.dot_general(a, b, (((1,), (1,)), ((), ())), preferred_element_type=F32)


def _tn_dot(a, b):
    return lax.dot_general(a, b, (((0,), (0,)), ((), ())), preferred_element_type=F32)


def _dot(a, b):
    return jnp.dot(a, b, preferred_element_type=F32)


def _dot_split(a, b_bf16, terms=3):
    acc = None
    rem = a
    for _ in range(terms):
        piece = rem.astype(BF16)
        part = _dot(piece, b_bf16)
        acc = part if acc is None else acc + part
        rem = rem - piece.astype(F32)
    return acc


def _cumsum_rows(x, tri):
    acc = None
    rem = x
    for _ in range(2):
        piece = rem.astype(BF16)
        part = _dot(tri, piece)
        acc = part if acc is None else acc + part
        rem = rem - piece.astype(F32)
    return acc


def _rms(x, gain):
    return x * lax.rsqrt(jnp.mean(x * x, axis=-1, keepdims=True) + NORM_EPS) * gain


def _sigmoid(x):
    return 0.5 + 0.5 * jnp.tanh(0.5 * x)


def _silu(x):
    return x * _sigmoid(x)


def _log_sigmoid(x):
    return jnp.minimum(x, 0.0) - jnp.log(1.0 + jnp.exp(-jnp.abs(x)))


def _gelu_tanh(x):
    c = math.sqrt(2.0 / math.pi)
    return 0.5 * x * (1.0 + jnp.tanh(c * (x + 0.044715 * (x * x * x))))


def _head_block_ones(dtype):
    r = lax.broadcasted_iota(jnp.int32, (BRANCH_W, BRANCH_W), 0) // HEAD_W
    c = lax.broadcasted_iota(jnp.int32, (BRANCH_W, BRANCH_W), 1) // HEAD_W
    return (r == c).astype(dtype)


def _head_stack_mask(m, dtype):
    r = lax.broadcasted_iota(jnp.int32, (N_HEADS * m, BRANCH_W), 0) // m
    c = lax.broadcasted_iota(jnp.int32, (N_HEADS * m, BRANCH_W), 1) // HEAD_W
    return (r == c).astype(dtype)


def _head_norm(o, gain, ones_bd):
    msq = _dot_split(o * o, ones_bd, terms=2) * (1.0 / HEAD_W)
    return o * lax.rsqrt(msq + NORM_EPS) * gain


def _chunk_lower_tri(n, chunk, dtype):
    r = lax.broadcasted_iota(jnp.int32, (n, n), 0)
    c = lax.broadcasted_iota(jnp.int32, (n, n), 1)
    return ((c <= r) & (r // chunk == c // chunk)).astype(dtype)


class _LayerParam(NamedTuple):
    array: jax.Array
    layer: int


def _const_spec(shape):
    zeros = (0,) * len(shape)
    return pl.BlockSpec(shape, lambda *_: zeros, pipeline_mode=pl.Buffered(1))


def _param_specs(consts):
    specs, operands = [], []
    for c in consts:
        if isinstance(c, _LayerParam):
            tail = c.array.shape[1:]
            idx = (c.layer,) + (0,) * len(tail)
            specs.append(pl.BlockSpec((None,) + tail, lambda *_, idx=idx: idx, pipeline_mode=pl.Buffered(1)))
            operands.append(c.array)
        else:
            specs.append(_const_spec(c.shape))
            operands.append(c)
    return specs, operands


def _front_kernel(x_ref, pos_ref, g_ref, wmix_ref, qn_ref, kvn_ref, wq_ref, wqr_ref, wk_ref, wvt_ref,
                  vones_ref, freq_ref, q_out, k_out, v_out, pb_out, pc_out, pd_lo_out, pd_hi_out):
    hn = _rms(x_ref[...], g_ref[...]).astype(BF16)
    pa = _dot(hn, wmix_ref[:, 0:W_SEC_A])
    pb_out[...] = _dot(hn, wmix_ref[:, W_SEC_A:W_SEC_A + W_SEC_B])
    pc_out[:, 0:W_SEC_C_MAIN] = _dot(hn, wmix_ref[:, W_SEC_A + W_SEC_B:W_SEC_A + W_SEC_B + W_SEC_C_MAIN])
    pc_out[:, W_SEC_C_MAIN:W_SEC_C] = pa[:, OFF_MLGATE:OFF_MLGATE + LANES]
    pd = _dot(hn, wmix_ref[:, W_SEC_A + W_SEC_B + W_SEC_C_MAIN:W_MIX])
    pd_lo_out[...] = pd[:, :LANES]
    pd_hi_out[...] = pd[:, LANES:]

    ang = pos_ref[...].astype(F32) * freq_ref[...]
    cos, sin = jnp.cos(ang), jnp.sin(ang)
    scale = (MLA_NOPE + MLA_ROPE) ** -0.5

    aqn = _rms(pa[:, OFF_AQ:OFF_AQ + MLA_Q_RANK], qn_ref[...]).astype(BF16)
    q0 = _dot(aqn, wq_ref[...])
    qr = _dot(aqn, wqr_ref[...])
    akvn = _rms(pa[:, OFF_AKV:OFF_AKV + MLA_KV_RANK], kvn_ref[...]).astype(BF16)
    kn = _dot(akvn, wk_ref[...])
    vt = _nt_dot(wvt_ref[...], akvn) + vones_ref[...]
    for t in range(v_out.shape[0]):
        v_out[t] = vt[:, t * ATT_BLK:(t + 1) * ATT_BLK].astype(v_out.dtype)
    k_rope = pa[:, OFF_KR:OFF_KR + LANES] * cos + pa[:, OFF_KRROT:OFF_KRROT + LANES] * sin
    for h in range(N_HEADS):
        sl = slice(h * LANES, (h + 1) * LANES)
        q_out[:, sl] = ((q0[:, sl] * cos + qr[:, sl] * sin) * scale).astype(q_out.dtype)
        k_out[:, sl] = (kn[:, sl] + k_rope).astype(k_out.dtype)


def _front(x2, pos2, lw, tm):
    n = x2.shape[0]
    row = lambda w: pl.BlockSpec((tm, w), lambda i: (i, 0))
    specs, consts = _param_specs([lw['g_mix_pre'], lw['w_mix'], lw['mla_q_norm'], lw['mla_kv_norm'], lw['wq'],
                                  lw['wqr'], lw['wk'], lw['wv_t'], lw['v_ones'], lw['freq']])
    return pl.pallas_call(
        _front_kernel,
        grid=(n // tm,),
        in_specs=[row(D_MODEL), row(1)] + specs,
        out_specs=[row(4 * LANES), row(4 * LANES),
                   pl.BlockSpec((tm // ATT_BLK, 4 * LANES, ATT_BLK), lambda i: (i, 0, 0)), row(W_SEC_B),
                   row(W_SEC_C), row(LANES), row(LANES)],
        out_shape=[jax.ShapeDtypeStruct((n, 4 * LANES), BF16), jax.ShapeDtypeStruct((n, 4 * LANES), BF16),
                   jax.ShapeDtypeStruct((n // ATT_BLK, 4 * LANES, ATT_BLK), BF16),
                   jax.ShapeDtypeStruct((n, W_SEC_B), F32),
                   jax.ShapeDtypeStruct((n, W_SEC_C), F32), jax.ShapeDtypeStruct((n, LANES), F32),
                   jax.ShapeDtypeStruct((n, LANES), F32)],
        compiler_params=pltpu.CompilerParams(dimension_semantics=("parallel",), vmem_limit_bytes=VMEM_LIMIT),
        name="front",
    )(x2, pos2, *consts)


def _mla_kernel(q_ref, k_ref, vt_ref, o_ref, *, blk):
    i = pl.program_id(1)
    key = lax.broadcasted_iota(jnp.int32, (blk, blk), 0)
    qry = lax.broadcasted_iota(jnp.int32, (blk, blk), 1)
    causal = key <= qry

    chains = [(s, slice(h * LANES, (h + 1) * LANES)) for s in range(q_ref.shape[0]) for h in range(N_HEADS)]

    def step(j, carry, masked):
        start = pl.multiple_of(j * blk, blk)
        scores = [_nt_dot(k_ref[s, pl.ds(start, blk), sl], q_ref[s, :, sl]) for s, sl in chains]
        if masked:
            scores = [jnp.where(causal, sc, MASK_VALUE) for sc in scores]
        m_new = [jnp.maximum(c[0], jnp.max(sc, axis=0, keepdims=True)) for c, sc in zip(carry, scores)]
        probs = [jnp.exp(sc - m).astype(BF16) for sc, m in zip(scores, m_new)]
        pv = [_dot(vt_ref[s, j, sl, :], p) for (s, sl), p in zip(chains, probs)]
        return tuple((m, jnp.exp(c[0] - m) * c[1] + x) for c, m, x in zip(carry, m_new, pv))

    init = tuple((jnp.full((1, blk), MASK_VALUE, F32), jnp.zeros((LANES, blk), F32)) for _ in chains)
    carry = lax.fori_loop(0, i, functools.partial(step, masked=False), init)
    carry = step(i, carry, True)
    outs = [acc[:HEAD_W] * (1.0 / acc[HEAD_W:HEAD_W + 1]) for _, acc in carry]
    for s in range(q_ref.shape[0]):
        o_ref[s] = jnp.concatenate(outs[s * N_HEADS:(s + 1) * N_HEADS], axis=0).T.astype(o_ref.dtype)


def _mla(q, k, vt, batch, seq):
    blk = ATT_BLK
    q3, k3 = (a.reshape(batch, seq, 4 * LANES) for a in (q, k))
    vt4 = vt.reshape(batch, seq // blk, 4 * LANES, blk)
    nseq = ATT_SEQS if batch % ATT_SEQS == 0 else 1
    out = pl.pallas_call(
        functools.partial(_mla_kernel, blk=blk),
        grid=(batch // nseq, seq // blk),
        in_specs=[pl.BlockSpec((nseq, blk, 4 * LANES), lambda b, i: (b, i, 0)),
                  pl.BlockSpec((nseq, seq, 4 * LANES), lambda b, i: (b, 0, 0)),
                  pl.BlockSpec((nseq, seq // blk, 4 * LANES, blk), lambda b, i: (b, 0, 0, 0))],
        out_specs=pl.BlockSpec((nseq, blk, BRANCH_W), lambda b, i: (b, i, 0)),
        out_shape=jax.ShapeDtypeStruct((batch, seq, BRANCH_W), BF16),
        compiler_params=pltpu.CompilerParams(dimension_semantics=("parallel", "arbitrary"),
                                             vmem_limit_bytes=VMEM_LIMIT),
        name="mla",
    )(q3, k3, vt4)
    return out.reshape(batch * seq, BRANCH_W)


def _hgrn_kernel(pb_ref, lbl_ref, gain_ref, o_ref, st_ref, q_sc, k_sc, b_sc, lk_sc, o_sc, *, layer, tb):
    @pl.when(pl.program_id(1) == 0)
    def _():
        st_ref[...] = jnp.zeros_like(st_ref)

    lg = lbl_ref[...]
    e = jnp.exp(lg - jnp.max(lg, axis=0, keepdims=True))
    sm = e / jnp.sum(e, axis=0, keepdims=True)
    lb = jnp.zeros((1, BRANCH_W), F32)
    for r in range(1, layer + 1):
        lb = lb + sm[r:r + 1, :]
    one_m_lb = 1.0 - lb

    ones_bd = _head_block_ones(BF16)
    ones_bd_f32 = _head_block_ones(F32)
    row_in_sub = lax.broadcasted_iota(jnp.int32, (HG_CHUNK, BRANCH_W), 0) % SUB

    f_logit = pb_ref[:, BRANCH_W:2 * BRANCH_W]
    e = jnp.exp(-jnp.abs(f_logit))
    r = 1.0 / (1.0 + e)
    nonneg = f_logit >= 0.0
    forget = lb + one_m_lb * jnp.where(nonneg, r, e * r)
    b_all = _cumsum_rows(jnp.log(jnp.maximum(forget, GATE_FLOOR)), _chunk_lower_tri(tb, HG_CHUNK, BF16))
    b_sc[...] = b_all
    k_sc[...] = one_m_lb * jnp.where(nonneg, e * r, r)
    lk_sc[...] = jnp.log(one_m_lb) - jnp.maximum(f_logit, 0.0) + jnp.log(r) - b_all
    q_sc[...] = _silu(pb_ref[:, 0:BRANCH_W])

    def chunk_body(c, carry):
        start = pl.multiple_of(c * HG_CHUNK, HG_CHUNK)
        rows = pl.ds(start, HG_CHUNK)
        q, k, b = q_sc[rows, :], k_sc[rows, :], b_sc[rows, :]
        v = pb_ref[rows, 2 * BRANCH_W:3 * BRANCH_W]
        b_end = b[HG_CHUNK - 1:HG_CHUNK, :]

        st = st_ref[...]
        o = _nt_dot((q * jnp.exp(b)).astype(BF16), st.astype(BF16))
        k_end = k * jnp.exp(b_end - b)
        st_ref[...] = (st * jnp.exp(b_end) + _tn_dot(v.astype(BF16), k_end.astype(BF16))) * ones_bd_f32

        m = SUB
        while m < HG_CHUNK:
            mask = _head_stack_mask(m, F32)
            parts = []
            for lo in range(0, HG_CHUNK, 2 * m):
                left = slice(lo, lo + m)
                right = slice(lo + m, lo + 2 * m)
                ref = b[lo + m - 1:lo + m, :]
                kt = k[left] * jnp.exp(ref - b[left])
                qt = q[right] * jnp.exp(b[right] - ref)
                k_hat = (jnp.concatenate([kt] * N_HEADS, axis=0) * mask).astype(BF16)
                v_hat = (jnp.concatenate([v[left]] * N_HEADS, axis=0) * mask).astype(BF16)
                a = _nt_dot(qt.astype(BF16), k_hat)
                parts.append(jnp.zeros((m, BRANCH_W), F32))
                parts.append(_dot(a.astype(BF16), v_hat))
            o = o + jnp.concatenate(parts, axis=0)
            m *= 2

        def key_row(ref, col, s):
            return jnp.concatenate(
                [jnp.broadcast_to(ref[pl.ds(start + blk * SUB + s, 1), col:col + BRANCH_W], (SUB, BRANCH_W))
                 for blk in range(HG_CHUNK // SUB)], axis=0)

        for s in range(SUB):
            expo = b + key_row(lk_sc, 0, s)
            if s > 0:
                expo = jnp.where(row_in_sub >= s, expo, MASK_VALUE)
            score = _dot((q * jnp.exp(expo)).astype(BF16), ones_bd)
            o = o + score * key_row(pb_ref, 2 * BRANCH_W, s)
        o_sc[rows, :] = o
        return carry

    lax.fori_loop(0, tb // HG_CHUNK, chunk_body, 0, unroll=True)

    gate = _silu(pb_ref[:, 3 * BRANCH_W:4 * BRANCH_W])
    o_ref[...] = (_head_norm(o_sc[...], gain_ref[...], ones_bd) * gate).astype(o_ref.dtype)


def _hgrn(pb, lb_logits, gain, layer, batch, seq, tb):
    pb3 = pb.reshape(batch, seq, W_SEC_B)
    specs, consts = _param_specs([lb_logits, gain])
    blk_scratch = pltpu.VMEM((tb, BRANCH_W), F32)
    out = pl.pallas_call(
        functools.partial(_hgrn_kernel, layer=layer, tb=tb),
        grid=(batch, seq // tb),
        in_specs=[pl.BlockSpec((None, tb, W_SEC_B), lambda b, i: (b, i, 0))] + specs,
        out_specs=pl.BlockSpec((None, tb, BRANCH_W), lambda b, i: (b, i, 0)),
        out_shape=jax.ShapeDtypeStruct((batch, seq, BRANCH_W), BF16),
        scratch_shapes=[pltpu.VMEM((BRANCH_W, BRANCH_W), F32)] + [blk_scratch] * 5,
        compiler_params=pltpu.CompilerParams(dimension_semantics=("parallel", "arbitrary"),
                                             vmem_limit_bytes=VMEM_LIMIT),
        name="hgrn",
    )(pb3, *consts)
    return out.reshape(batch * seq, BRANCH_W)


def _mlstm_kernel(pc_ref, cw_ref, cb_ref, wq_ref, wk_ref, ib_ref, fb_ref, gain_ref, o_ref,
                  xext_ref, q_sc, k_sc, il_sc, b_sc, h_sc, c_ref, n_ref, m_ref, *, tb):
    pad = 8
    seqs = range(pc_ref.shape[0])

    @pl.when(pl.program_id(1) == 0)
    def _():
        xext_ref[:, 0:pad, :] = jnp.zeros((len(seqs), pad, BRANCH_W), F32)
        c_ref[...] = jnp.zeros_like(c_ref)
        n_ref[...] = jnp.zeros_like(n_ref)
        m_ref[...] = jnp.zeros_like(m_ref)

    gr = lax.broadcasted_iota(jnp.int32, (LANES, BRANCH_W), 0)
    gc = lax.broadcasted_iota(jnp.int32, (LANES, BRANCH_W), 1) // HEAD_W
    tri = _chunk_lower_tri(tb, CHUNK, BF16)
    for s in seqs:
        xext_ref[s, pad:pad + tb, :] = pc_ref[s, :, 0:BRANCH_W]
        conv = jnp.zeros((tb, BRANCH_W), F32) + cb_ref[...]
        for j in range(ML_CONV):
            first = pad - (ML_CONV - 1) + j
            conv = conv + xext_ref[s, first:first + tb, :] * cw_ref[j:j + 1, :]
        xext_ref[s, 0:pad, :] = xext_ref[s, tb:tb + pad, :]
        xc = _silu(conv).astype(BF16)
        q_sc[s] = _dot(xc, wq_ref[...])
        k_sc[s] = _dot(xc, wk_ref[...]) * (HEAD_W ** -0.5)
        gates = pc_ref[s, :, 3 * BRANCH_W:3 * BRANCH_W + LANES]
        il_sc[s] = _dot_split(gates, (gr == gc).astype(BF16)) + ib_ref[...]
        f_log = _log_sigmoid(_dot_split(gates, (gr == gc + N_HEADS).astype(BF16)) + fb_ref[...])
        b_sc[s] = _cumsum_rows(f_log, tri)

    ones_bd = _head_block_ones(BF16)
    ones_bd_f32 = _head_block_ones(F32)
    stack_mask = _head_stack_mask(CHUNK, F32)
    lane = lax.broadcasted_iota(jnp.int32, (CHUNK, BRANCH_W), 1)
    rowi = lax.broadcasted_iota(jnp.int32, (CHUNK, BRANCH_W), 0)
    diag_sel = (lane % HEAD_W == rowi).astype(F32)
    causal = (lane % HEAD_W) <= rowi

    def head_max(a):
        out = jnp.zeros_like(a)
        for h in range(N_HEADS):
            mx = jnp.max(a[:, h * HEAD_W:(h + 1) * HEAD_W], axis=-1, keepdims=True)
            out = jnp.where(lane // HEAD_W == h, mx, out)
        return out

    def chunk_body(c, carry):
        rows = pl.ds(pl.multiple_of(c * CHUNK, CHUNK), CHUNK)
        q = [q_sc[s, rows, :] for s in seqs]
        k = [k_sc[s, rows, :] for s in seqs]
        i_log = [il_sc[s, rows, :] for s in seqs]
        b = [b_sc[s, rows, :] for s in seqs]
        v = [pc_ref[s, rows, BRANCH_W:2 * BRANCH_W] for s in seqs]
        qb = [a.astype(BF16) for a in q]
        k_hat = [(jnp.concatenate([a] * N_HEADS, axis=0) * stack_mask).astype(BF16) for a in k]
        v_hat = [(jnp.concatenate([a] * N_HEADS, axis=0) * stack_mask).astype(BF16) for a in v]
        scores = [_nt_dot(qb[s], k_hat[s]) for s in seqs]
        c0, n0, m0 = [c_ref[s] for s in seqs], [n_ref[s] for s in seqs], [m_ref[s] for s in seqs]
        inter_num = [_dot(qb[s], c0[s].astype(BF16)) for s in seqs]
        inter_den = [_dot((q[s] * n0[s]).astype(BF16), ones_bd) for s in seqs]

        b_end = [a[CHUNK - 1:CHUNK, :] for a in b]
        g_end = [b_end[s] - b[s] + i_log[s] for s in seqs]
        m_new = [jnp.maximum(b_end[s] + m0[s], jnp.max(g_end[s], axis=0, keepdims=True)) for s in seqs]
        kw = [k[s] * jnp.exp(g_end[s] - m_new[s]) for s in seqs]
        kv = [_tn_dot(kw[s].astype(BF16), v[s].astype(BF16)) for s in seqs]

        b_row = [jnp.sum(a * diag_sel, axis=0, keepdims=True) for a in b]
        i_row = [jnp.sum(a * diag_sel, axis=0, keepdims=True) for a in i_log]
        dmat = [jnp.where(causal, b[s] - b_row[s] + i_row[s], MASK_VALUE) for s in seqs]
        a_t = [b[s] + m0[s] for s in seqs]
        m_t = [jnp.maximum(a_t[s], head_max(dmat[s])) for s in seqs]
        w_inter = [jnp.exp(a_t[s] - m_t[s]) for s in seqs]
        qk = [(scores[s] * jnp.exp(dmat[s] - m_t[s])).astype(BF16) for s in seqs]
        num = [_dot(qk[s], v_hat[s]) for s in seqs]
        den = [_dot(qk[s], ones_bd) for s in seqs]
        for s in seqs:
            h_num = num[s] + w_inter[s] * inter_num[s]
            h_den = den[s] + w_inter[s] * inter_den[s]
            h_sc[s, rows, :] = h_num / jnp.maximum(jnp.abs(h_den), jnp.exp(-m_t[s]))
            keep = jnp.exp(b_end[s] + m0[s] - m_new[s])
            c_ref[s] = (keep * c0[s] + kv[s]) * ones_bd_f32
            n_ref[s] = keep * n0[s] + jnp.sum(kw[s], axis=0, keepdims=True)
            m_ref[s] = m_new[s]
        return carry

    lax.fori_loop(0, tb // CHUNK, chunk_body, 0, unroll=2)

    for s in seqs:
        out_gate = _sigmoid(pc_ref[s, :, 2 * BRANCH_W:3 * BRANCH_W])
        o_ref[s] = (out_gate * _head_norm(h_sc[s], gain_ref[...], ones_bd)).astype(o_ref.dtype)


def _mlstm(pc, lw, batch, seq, tb):
    pc3 = pc.reshape(batch, seq, W_SEC_C)
    specs, consts = _param_specs([lw['ml_conv_w'], lw['ml_conv_b'], lw['ml_wq_bd'], lw['ml_wk_bd'],
                                  lw['ml_i_bias'], lw['ml_f_bias'], lw['ml_out_norm']])
    nseq = MIX_SEQS if batch % MIX_SEQS == 0 else 1
    blk_scratch = pltpu.VMEM((nseq, tb, BRANCH_W), F32)
    out = pl.pallas_call(
        functools.partial(_mlstm_kernel, tb=tb),
        grid=(batch // nseq, seq // tb),
        in_specs=[pl.BlockSpec((nseq, tb, W_SEC_C), lambda b, i: (b, i, 0))] + specs,
        out_specs=pl.BlockSpec((nseq, tb, BRANCH_W), lambda b, i: (b, i, 0)),
        out_shape=jax.ShapeDtypeStruct((batch, seq, BRANCH_W), BF16),
        scratch_shapes=[pltpu.VMEM((nseq, tb + 8, BRANCH_W), F32)] + [blk_scratch] * 5
                       + [pltpu.VMEM((nseq, BRANCH_W, BRANCH_W), F32), pltpu.VMEM((nseq, 1, BRANCH_W), F32),
                          pltpu.VMEM((nseq, 1, BRANCH_W), F32)],
        compiler_params=pltpu.CompilerParams(dimension_semantics=("parallel", "arbitrary"),
                                             vmem_limit_bytes=VMEM_LIMIT),
        name="mlstm",
    )(pc3, *consts)
    return out.reshape(batch * seq, BRANCH_W)


def _lane_block_transpose(rows):
    n = len(rows)
    width = rows[0].shape[1]
    blk = lax.broadcasted_iota(jnp.int32, rows[0].shape, 1) // (width // n)
    d = n // 2
    while d >= 1:
        upper = (blk & d) != 0
        shift = d * (width // n)
        nxt = list(rows)
        for i in range(n):
            if i & d == 0:
                lo, hi = rows[i], rows[i + d]
                nxt[i] = jnp.where(upper, pltpu.roll(hi, shift, 1), lo)
                nxt[i + d] = jnp.where(upper, hi, pltpu.roll(lo, width - shift, 1))
        rows = nxt
        d //= 2
    return rows


def _s5_kernel(u_lo_ref, u_hi_ref, tz_ref, min_re_ref, min_im_ref, mout_re_ref, mout_im_ref, pw_re_ref,
               pw_im_ref, d_ref, y_lo_ref, y_hi_ref, ug_ref, yg_ref, *, n_rows):
    nb = LANES // S5_GROUP_CH
    for tok_half in range(S5_CHUNK // nb):
        for grp_half, u_ref in enumerate((u_lo_ref, u_hi_ref)):
            by_token = [u_ref[pl.ds(nb * tok_half + t, n_rows, stride=S5_CHUNK), :] for t in range(nb)]
            by_group = _lane_block_transpose(by_token)
            for g in range(nb):
                ug_ref[nb * grp_half + g, :, tok_half * LANES:(tok_half + 1) * LANES] = by_group[g]

    row = lax.broadcasted_iota(jnp.int32, (n_rows, LANES), 0)
    for pair in range(S5_GROUPS // 2):
        u = [ug_ref[2 * pair + s] for s in range(2)]
        ub = [a.astype(BF16) for a in u]
        s_re = _dot(ub[0], min_re_ref[2 * pair]) + _dot(ub[1], min_re_ref[2 * pair + 1])
        s_im = _dot(ub[0], min_im_ref[2 * pair]) + _dot(ub[1], min_im_ref[2 * pair + 1])
        step, k = 1, 0
        while step < n_rows:
            keep = row >= step
            p_re = jnp.where(keep, pltpu.roll(s_re, step, 0), 0.0)
            p_im = jnp.where(keep, pltpu.roll(s_im, step, 0), 0.0)
            a_re = pw_re_ref[pair, k:k + 1, :]
            a_im = pw_im_ref[pair, k:k + 1, :]
            s_re, s_im = s_re + a_re * p_re - a_im * p_im, s_im + a_re * p_im + a_im * p_re
            step, k = step * 2, k + 1
        first = row >= 1
        s0_re = jnp.where(first, pltpu.roll(s_re, 1, 0), 0.0).astype(BF16)
        s0_im = jnp.where(first, pltpu.roll(s_im, 1, 0), 0.0).astype(BF16)
        for s in range(2):
            g = 2 * pair + s
            yg_ref[g] = (_dot(ub[s], tz_ref[g]) + _dot(s0_re, mout_re_ref[g]) + _dot(s0_im, mout_im_ref[g])
                         + u[s] * d_ref[g])

    for tok_half in range(S5_CHUNK // nb):
        for grp_half, y_ref in enumerate((y_lo_ref, y_hi_ref)):
            by_group = [yg_ref[nb * grp_half + g, :, tok_half * LANES:(tok_half + 1) * LANES] for g in range(nb)]
            by_token = _lane_block_transpose(by_group)
            for t in range(nb):
                y_ref[pl.ds(nb * tok_half + t, n_rows, stride=S5_CHUNK), :] = by_token[t]


def _s5(pd_lo, pd_hi, lw, batch, seq):
    n_rows = seq // S5_CHUNK
    assert n_rows <= 2 ** S5_SCAN_STEPS
    width = S5_CHUNK * S5_GROUP_CH
    specs, consts = _param_specs([lw['s5_tz'], lw['s5_min_re'], lw['s5_min_im'], lw['s5_mout_re'],
                                  lw['s5_mout_im'], lw['s5_pw_re'], lw['s5_pw_im'], lw['s5_d']])
    seq_blk = pl.BlockSpec((None, seq, LANES), lambda b: (b, 0, 0))
    out = jax.ShapeDtypeStruct((batch, seq, LANES), F32)
    y_lo, y_hi = pl.pallas_call(
        functools.partial(_s5_kernel, n_rows=n_rows),
        grid=(batch,),
        in_specs=[seq_blk, seq_blk] + specs,
        out_specs=[seq_blk, seq_blk],
        out_shape=[out, out],
        scratch_shapes=[pltpu.VMEM((S5_GROUPS, n_rows, width), F32), pltpu.VMEM((S5_CHUNK, n_rows, BRANCH_W), F32)],
        compiler_params=pltpu.CompilerParams(dimension_semantics=("parallel",), vmem_limit_bytes=VMEM_LIMIT),
        name="s5",
    )(pd_lo.reshape(batch, seq, LANES), pd_hi.reshape(batch, seq, LANES), *consts)
    return y_lo.reshape(batch * seq, LANES), y_hi.reshape(batch * seq, LANES)


def _merge_kernel(x_ref, a_ref, b_ref, c_ref, y_lo_ref, y_hi_ref, gpre_ref, wgate_ref, wbr_ref, wout_ref,
                  wglu_ref, bglu_ref, gpost_ref, o_ref):
    x = x_ref[...]
    hn = _rms(x, gpre_ref[...]).astype(BF16)
    y = _gelu_tanh(jnp.concatenate([y_lo_ref[...], y_hi_ref[...]], axis=-1))
    out_d = (y * _sigmoid(_dot(y.astype(BF16), wglu_ref[...]) + bglu_ref[...])).astype(BF16)
    branches = (a_ref[...], b_ref[...], c_ref[...], out_d)
    merged = None
    for n in range(4):
        gate = _sigmoid(_dot(hn, wgate_ref[:, n * D_MODEL:(n + 1) * D_MODEL]))
        term = gate * _dot(branches[n], wbr_ref[n])
        merged = term if merged is None else merged + term
    mix = _dot(merged.astype(BF16), wout_ref[...])
    o_ref[...] = x + _rms(mix, gpost_ref[...])


def _merge(x2, out_a, out_b, out_c, y_lo, y_hi, lw, tm):
    n = x2.shape[0]
    row = lambda w: pl.BlockSpec((tm, w), lambda i: (i, 0))
    specs, consts = _param_specs([lw['g_mix_pre'], lw['w_gate'], lw['w_branch'], lw['w_out'], lw['s5_w_glu'],
                                  lw['s5_b_glu'], lw['g_mix_post']])
    return pl.pallas_call(
        _merge_kernel,
        grid=(n // tm,),
        in_specs=[row(D_MODEL), row(BRANCH_W), row(BRANCH_W), row(BRANCH_W), row(LANES), row(LANES)] + specs,
        out_specs=row(D_MODEL),
        out_shape=jax.ShapeDtypeStruct((n, D_MODEL), F32),
        compiler_params=pltpu.CompilerParams(dimension_semantics=("parallel",), vmem_limit_bytes=VMEM_LIMIT),
        name="merge",
    )(x2, out_a, out_b, out_c, y_lo, y_hi, *consts)


def _memkv_kernel(mem_ref, g_ref, wk_ref, wv_ref, k_out, v_out):
    mn = _rms(mem_ref[...], g_ref[...]).astype(BF16)
    k_out[...] = _dot(mn, wk_ref[...]).astype(k_out.dtype)
    v_out[...] = _dot(mn, wv_ref[...]).astype(v_out.dtype)


def _memkv(mem, lw):
    batch, mlen, _ = mem.shape
    specs, consts = _param_specs([lw['g_mem'], lw['xa_wk'], lw['xa_wv']])
    blk = pl.BlockSpec((None, mlen, D_MODEL), lambda b: (b, 0, 0))
    return pl.pallas_call(
        _memkv_kernel,
        grid=(batch,),
        in_specs=[blk] + specs,
        out_specs=[blk, blk],
        out_shape=[jax.ShapeDtypeStruct((batch, mlen, D_MODEL), BF16)] * 2,
        compiler_params=pltpu.CompilerParams(dimension_semantics=("parallel",), vmem_limit_bytes=VMEM_LIMIT),
        name="memkv",
    )(mem, *consts)


def _xattn_kernel(x_ref, k_ref, v_ref, gpre_ref, wq_ref, wo_ref, gpost_ref, o_ref):
    x = x_ref[...]
    hn = _rms(x, gpre_ref[...]).astype(BF16)
    q = (_dot(hn, wq_ref[...]) * (XA_DIM ** -0.5)).astype(BF16)
    outs = []
    for h in range(XA_HEADS):
        sl = slice(h * XA_DIM, (h + 1) * XA_DIM)
        s = _nt_dot(q[:, sl], k_ref[:, sl])
        p = jnp.exp(s - jnp.max(s, axis=-1, keepdims=True))
        inv = 1.0 / jnp.sum(p, axis=-1, keepdims=True)
        outs.append((_dot(p.astype(BF16), v_ref[:, sl]) * inv).astype(BF16))
    xa = _dot(jnp.concatenate(outs, axis=-1), wo_ref[...])
    o_ref[...] = x + _rms(xa, gpost_ref[...])


def _xattn(x2, mem_k, mem_v, lw, batch, seq, tm):
    mlen = mem_k.shape[1]
    x3 = x2.reshape(batch, seq, D_MODEL)
    specs, consts = _param_specs([lw['g_xa_pre'], lw['xa_wq'], lw['xa_wo'], lw['g_xa_post']])
    row = pl.BlockSpec((None, tm, D_MODEL), lambda b, i: (b, i, 0))
    kv = pl.BlockSpec((None, mlen, D_MODEL), lambda b, i: (b, 0, 0))
    out = pl.pallas_call(
        _xattn_kernel,
        grid=(batch, seq // tm),
        in_specs=[row, kv, kv] + specs,
        out_specs=row,
        out_shape=jax.ShapeDtypeStruct((batch, seq, D_MODEL), F32),
        compiler_params=pltpu.CompilerParams(dimension_semantics=("parallel", "parallel"),
                                             vmem_limit_bytes=VMEM_LIMIT),
        name="xattn",
    )(x3, mem_k, mem_v, *consts)
    return out.reshape(batch * seq, D_MODEL)


def _ffn_kernel(x_ref, gpre_ref, win_ref, wo_ref, gpost_ref, o_ref):
    x = x_ref[...]
    hn = _rms(x, gpre_ref[...]).astype(BF16)
    acc = None
    for j in range(D_FF // FF_CHUNK):
        lo = j * FF_CHUNK
        a = _dot(hn, win_ref[:, lo:lo + FF_CHUNK])
        b = _dot(hn, win_ref[:, D_FF + lo:D_FF + lo + FF_CHUNK])
        part = _dot((_silu(a) * b).astype(BF16), wo_ref[lo:lo + FF_CHUNK, :])
        acc = part if acc is None else acc + part
    o_ref[...] = x + _rms(acc, gpost_ref[...])


def _ffn(x2, lw, tm):
    n = x2.shape[0]
    row = pl.BlockSpec((tm, D_MODEL), lambda i: (i, 0))
    specs, consts = _param_specs([lw['g_ffn_pre'], lw['ffn_w_in'], lw['ffn_wo'], lw['g_ffn_post']])
    return pl.pallas_call(
        _ffn_kernel,
        grid=(n // tm,),
        in_specs=[row] + specs,
        out_specs=row,
        out_shape=jax.ShapeDtypeStruct((n, D_MODEL), F32),
        compiler_params=pltpu.CompilerParams(dimension_semantics=("parallel",), vmem_limit_bytes=VMEM_LIMIT),
        name="ffn",
    )(x2, *consts)


def _rot_half(w):
    half = MLA_ROPE // 2
    return jnp.concatenate([-w[..., half:], w[..., :half]], axis=-1)


def _block_diag(w):
    depth, h, d, e = w.shape
    eye = jnp.eye(h, dtype=w.dtype)
    return (eye[None, :, None, :, None] * w[:, :, :, None, :]).reshape(depth, h * d, h * e)


def _s5_tables(a_re, a_im, log_dt, b_re, b_im, c_re, c_im, d):
    hi = lax.Precision.HIGHEST
    g, p, hch = b_re.shape
    L = S5_CHUNK
    width = L * hch
    dt = jnp.exp(log_dt)[:, None]
    lam_re, lam_im = a_re * dt, a_im * dt
    mag = jnp.exp(lam_re)
    ab_re, ab_im = mag * jnp.cos(lam_im), mag * jnp.sin(lam_im)
    inv_abs2 = 1.0 / (a_re * a_re + a_im * a_im)
    z_re = ((ab_re - 1.0) * a_re + ab_im * a_im) * inv_abs2
    z_im = (ab_im * a_re - (ab_re - 1.0) * a_im) * inv_abs2
    bt_re, bt_im = b_re.swapaxes(1, 2), b_im.swapaxes(1, 2)
    bbt_re = z_re[:, None, :] * bt_re - z_im[:, None, :] * bt_im
    bbt_im = z_re[:, None, :] * bt_im + z_im[:, None, :] * bt_re

    def power(k):
        k = jnp.asarray(k, F32)[None, :, None]
        m = jnp.exp(k * lam_re[:, None, :])
        return m * jnp.cos(k * lam_im[:, None, :]), m * jnp.sin(k * lam_im[:, None, :])

    pw_re, pw_im = power(jnp.arange(L + 1))
    ct_re, ct_im = c_re.swapaxes(1, 2)[:, :, None, :], c_im.swapaxes(1, 2)[:, :, None, :]
    pk_re, pk_im = pw_re.swapaxes(1, 2)[:, :, :, None], pw_im.swapaxes(1, 2)[:, :, :, None]
    cp_re = (ct_re * pk_re - ct_im * pk_im).reshape(g, p, (L + 1) * hch)
    cp_im = (ct_re * pk_im + ct_im * pk_re).reshape(g, p, (L + 1) * hch)
    kern = (jnp.einsum('ghp,gpx->ghx', bbt_re, cp_re, precision=hi)
            - jnp.einsum('ghp,gpx->ghx', bbt_im, cp_im, precision=hi))
    x = jnp.arange((L + 1) * hch)
    y = jnp.arange(width)
    lag = y[None, None, :] // hch - jnp.arange(L)[:, None, None]
    place = ((x[None, :, None] // hch == lag) & (x[None, :, None] % hch == y[None, None, :] % hch)).astype(BF16)
    tz = jnp.einsum('ghx,lxy->glhy', kern.astype(BF16), place, preferred_element_type=F32)
    tz = tz.astype(BF16).reshape(g, width, width)

    rp_re, rp_im = pw_re[:, L - 1 - jnp.arange(L), None, :], pw_im[:, L - 1 - jnp.arange(L), None, :]
    min_re = (rp_re * bbt_re[:, None] - rp_im * bbt_im[:, None]).reshape(g, width, p)
    min_im = (rp_re * bbt_im[:, None] + rp_im * bbt_re[:, None]).reshape(g, width, p)
    mout_re, mout_im = cp_re[:, :, hch:], -cp_im[:, :, hch:]

    even_group = (jnp.arange(g) % 2 == 0)[:, None, None]

    def side_by_side(a, axis):
        z = jnp.zeros_like(a)
        return jnp.where(even_group, jnp.concatenate([a, z], axis=axis), jnp.concatenate([z, a], axis=axis))

    st_re, st_im = power(L * (2 ** jnp.arange(S5_SCAN_STEPS)))
    pair_lanes = lambda a: a.reshape(g // 2, 2, S5_SCAN_STEPS, p).swapaxes(1, 2).reshape(g // 2, S5_SCAN_STEPS, 2 * p)
    return dict(
        s5_tz=tz,
        s5_min_re=side_by_side(min_re, 2).astype(BF16), s5_min_im=side_by_side(min_im, 2).astype(BF16),
        s5_mout_re=side_by_side(mout_re, 1).astype(BF16), s5_mout_im=side_by_side(mout_im, 1).astype(BF16),
        s5_pw_re=pair_lanes(st_re), s5_pw_im=pair_lanes(st_im),
        s5_d=jnp.tile(d, (1, L)).reshape(g, 1, width),
    )


def _stacked_weights(p):
    depth = p['w_in'].shape[0]
    w_in = p['w_in'].astype(BF16)
    zeros = lambda n: jnp.zeros((depth, D_MODEL, n), BF16)
    kr = w_in[:, :, IN_KR:IN_B]
    rope_pad = LANES - MLA_NOPE - MLA_ROPE
    w_mix = jnp.concatenate([
        w_in[:, :, 0:IN_KR],
        zeros(MLA_NOPE), kr, zeros(rope_pad),
        zeros(MLA_NOPE), _rot_half(kr), zeros(rope_pad),
        w_in[:, :, IN_CG:IN_D], zeros(LANES - 2 * N_HEADS),
        w_in[:, :, IN_B:IN_C],
        w_in[:, :, IN_C:IN_CG],
        w_in[:, :, IN_D:IN_GATE]], axis=2)
    assert w_mix.shape[2] == W_MIX

    uq = p['mla_w_uq'].astype(BF16).reshape(depth, MLA_Q_RANK, N_HEADS, MLA_NOPE + MLA_ROPE)
    zq = jnp.zeros((depth, MLA_Q_RANK, N_HEADS, rope_pad), BF16)
    wq = jnp.concatenate([uq, zq], axis=-1).reshape(depth, MLA_Q_RANK, N_HEADS * LANES)
    wqr = jnp.concatenate([jnp.zeros((depth, MLA_Q_RANK, N_HEADS, MLA_NOPE), BF16),
                           _rot_half(uq[..., MLA_NOPE:]), zq], axis=-1).reshape(depth, MLA_Q_RANK, N_HEADS * LANES)
    ukv = p['mla_w_ukv'].astype(BF16).reshape(depth, MLA_KV_RANK, N_HEADS, MLA_NOPE + HEAD_W)
    zk = jnp.zeros((depth, MLA_KV_RANK, N_HEADS, LANES - MLA_NOPE), BF16)
    wk = jnp.concatenate([ukv[..., :MLA_NOPE], zk], axis=-1).reshape(depth, MLA_KV_RANK, N_HEADS * LANES)
    wv_t = jnp.concatenate([ukv[..., MLA_NOPE:], zk], axis=-1).reshape(depth, MLA_KV_RANK, N_HEADS * LANES)
    wv_t = wv_t.swapaxes(1, 2)

    spread = lambda b: jnp.repeat(b, HEAD_W, axis=-1)[:, None, :]
    row = lambda a: a[:, None, :]
    sw = dict(
        g_mix_pre=row(p['norm_mix_pre']), g_mix_post=row(p['norm_mix_post']),
        w_mix=w_mix, w_gate=w_in[:, :, IN_GATE:],
        mla_q_norm=row(p['mla_q_norm']), mla_kv_norm=row(p['mla_kv_norm']),
        wq=wq, wqr=wqr, wk=wk, wv_t=wv_t,
        hg_out_norm=row(p['hg_out_norm']),
        ml_conv_w=p['ml_conv_w'], ml_conv_b=row(p['ml_conv_b']),
        ml_wq_bd=_block_diag(p['ml_w_q'].astype(BF16)), ml_wk_bd=_block_diag(p['ml_w_k'].astype(BF16)),
        ml_i_bias=spread(p['ml_i_bias']), ml_f_bias=spread(p['ml_f_bias']),
        ml_out_norm=row(p['ml_out_norm']),
        s5_w_glu=p['s5_w_glu'].astype(BF16), s5_b_glu=row(p['s5_b_glu']),
        w_branch=p['w_branch'].astype(BF16), w_out=p['w_out'].astype(BF16),
        g_xa_pre=row(p['norm_xa_pre']), g_xa_post=row(p['norm_xa_post']), g_mem=row(p['norm_mem']),
        xa_wq=p['xa_wq'].astype(BF16), xa_wk=p['xa_wk'].astype(BF16),
        xa_wv=p['xa_wv'].astype(BF16), xa_wo=p['xa_wo'].astype(BF16),
        g_ffn_pre=row(p['norm_ffn_pre']), g_ffn_post=row(p['norm_ffn_post']),
        ffn_w_in=p['ffn_w_in'].astype(BF16), ffn_wo=p['ffn_w_out'].astype(BF16),
    )
    sw.update(jax.vmap(_s5_tables)(p['s5_a_re'], p['s5_a_im'], p['s5_log_dt'], p['s5_b_re'], p['s5_b_im'],
                                   p['s5_c_re'], p['s5_c_im'], p['s5_d']))
    return sw


def _shared_constants():
    half = MLA_ROPE // 2
    inv_freq = ROPE_THETA ** (-np.arange(half, dtype=np.float32) / half)
    freq = np.zeros((1, LANES), np.float32)
    freq[0, MLA_NOPE:MLA_NOPE + half] = inv_freq
    freq[0, MLA_NOPE + half:MLA_NOPE + MLA_ROPE] = inv_freq
    v_ones = np.zeros((N_HEADS * LANES, 1), np.float32)
    v_ones[HEAD_W::LANES] = 1.0
    return dict(freq=jnp.asarray(freq), v_ones=jnp.asarray(v_ones))


def kernel(x, mem, positions, norm_mix_pre, norm_mix_post, w_in, mla_q_norm, mla_w_uq, mla_kv_norm, mla_w_ukv, hg_lb_logits, hg_out_norm, ml_conv_w, ml_conv_b, ml_w_q, ml_w_k, ml_i_bias, ml_f_bias, ml_out_norm, s5_a_re, s5_a_im, s5_log_dt, s5_b_re, s5_b_im, s5_c_re, s5_c_im, s5_d, s5_w_glu, s5_b_glu, w_branch, w_out, norm_xa_pre, norm_xa_post, norm_mem, xa_wq, xa_wk, xa_wv, xa_wo, norm_ffn_pre, norm_ffn_post, ffn_w_in, ffn_w_out):
    p = dict(locals())
    batch, seq, _ = x.shape
    depth = w_in.shape[0]
    n = batch * seq
    tm = min(ROW_TILE, seq)
    tb = min(SEQ_TILE, seq)
    x2 = x.reshape(n, D_MODEL)
    pos2 = positions.reshape(n, 1).astype(jnp.int32)
    stacked = _stacked_weights(p)
    shared = _shared_constants()
    for l in range(depth):
        lw = {name: _LayerParam(a, l) for name, a in stacked.items()}
        lw.update(shared)
        q, k, vt, pb, pc, pd_lo, pd_hi = _front(x2, pos2, lw, tm)
        out_a = _mla(q, k, vt, batch, seq)
        out_b = _hgrn(pb, hg_lb_logits, lw['hg_out_norm'], l, batch, seq, tb)
        out_c = _mlstm(pc, lw, batch, seq, tb)
        y_lo, y_hi = _s5(pd_lo, pd_hi, lw, batch, seq)
        x2 = _merge(x2, out_a, out_b, out_c, y_lo, y_hi, lw, tm)
        mem_k, mem_v = _memkv(mem, lw)
        x2 = _xattn(x2, mem_k, mem_v, lw, batch, seq, tm)
        x2 = _ffn(x2, lw, tm)
    return x2.reshape(batch, seq, D_MODEL)
```

```python
import functools
import math
from typing import NamedTuple

import numpy as np
import jax
import jax.numpy as jnp
from jax import lax
from jax.experimental import pallas as pl
from jax.experimental.pallas import tpu as pltpu

F32 = jnp.float32
BF16 = jnp.bfloat16

D_MODEL = 1024
NORM_EPS = 1e-6
MASK_VALUE = -1e30
GATE_FLOOR = 1e-30
BRANCH_W = 256
N_HEADS = 4
HEAD_W = 64
MLA_Q_RANK = 256
MLA_KV_RANK = 128
MLA_NOPE = 64
MLA_ROPE = 32
ROPE_THETA = 10000.0
ML_CONV = 4
S5_GROUPS = 16
S5_GROUP_CH = 16
S5_STATE = 64
S5_CHUNK = 16
S5_SCAN_STEPS = 8
XA_HEADS = 4
XA_DIM = 256
D_FF = 2816
FF_CHUNK = 256

LANES = 128
ATT_BLK = 256
MIX_SEQS = 2
ATT_SEQS = 4
CHUNK = 64
HG_CHUNK = 128
SUB = 16
ROW_TILE = 1024
SEQ_TILE = 512
VMEM_LIMIT = 56 * 1024 * 1024

OFF_AQ, OFF_AKV, OFF_KR, OFF_KRROT, OFF_MLGATE, W_SEC_A = 0, 256, 384, 512, 640, 768
W_SEC_B = 4 * BRANCH_W
W_SEC_C_MAIN = 3 * BRANCH_W
W_SEC_C = W_SEC_C_MAIN + LANES
W_SEC_D = BRANCH_W
W_MIX = W_SEC_A + W_SEC_B + W_SEC_C_MAIN + W_SEC_D
IN_KR, IN_B, IN_C, IN_CG, IN_D, IN_GATE = 384, 416, 1440, 2208, 2216, 2472


def _nt_dot(a, b):
    return lax.dot_general(a, b, (((1,), (1,)), ((), ())), preferred_element_type=F32)


def _tn_dot(a, b):
    return lax.dot_general(a, b, (((0,), (0,)), ((), ())), preferred_element_type=F32)


def _dot(a, b):
    return jnp.dot(a, b, preferred_element_type=F32)


def _dot_split(a, b_bf16, terms=3):
    acc = None
    rem = a
    for _ in range(terms):
        piece = rem.astype(BF16)
        part = _dot(piece, b_bf16)
        acc = part if acc is None else acc + part
        rem = rem - piece.astype(F32)
    return acc


def _cumsum_rows(x, tri):
    acc = None
    rem = x
    for _ in range(2):
        piece = rem.astype(BF16)
        part = _dot(tri, piece)
        acc = part if acc is None else acc + part
        rem = rem - piece.astype(F32)
    return acc


def _rms(x, gain):
    return x * lax.rsqrt(jnp.mean(x * x, axis=-1, keepdims=True) + NORM_EPS) * gain


def _sigmoid(x):
    return 0.5 + 0.5 * jnp.tanh(0.5 * x)


def _silu(x):
    return x * _sigmoid(x)


def _log_sigmoid(x):
    return jnp.minimum(x, 0.0) - jnp.log(1.0 + jnp.exp(-jnp.abs(x)))


def _gelu_tanh(x):
    c = math.sqrt(2.0 / math.pi)
    return 0.5 * x * (1.0 + jnp.tanh(c * (x + 0.044715 * (x * x * x))))


def _head_block_ones(dtype):
    r = lax.broadcasted_iota(jnp.int32, (BRANCH_W, BRANCH_W), 0) // HEAD_W
    c = lax.broadcasted_iota(jnp.int32, (BRANCH_W, BRANCH_W), 1) // HEAD_W
    return (r == c).astype(dtype)


def _head_stack_mask(m, dtype):
    r = lax.broadcasted_iota(jnp.int32, (N_HEADS * m, BRANCH_W), 0) // m
    c = lax.broadcasted_iota(jnp.int32, (N_HEADS * m, BRANCH_W), 1) // HEAD_W
    return (r == c).astype(dtype)


def _head_norm(o, gain, ones_bd):
    msq = _dot_split(o * o, ones_bd, terms=2) * (1.0 / HEAD_W)
    return o * lax.rsqrt(msq + NORM_EPS) * gain


def _chunk_lower_tri(n, chunk, dtype):
    r = lax.broadcasted_iota(jnp.int32, (n, n), 0)
    c = lax.broadcasted_iota(jnp.int32, (n, n), 1)
    return ((c <= r) & (r // chunk == c // chunk)).astype(dtype)


class _LayerParam(NamedTuple):
    array: jax.Array
    layer: int


def _const_spec(shape):
    zeros = (0,) * len(shape)
    return pl.BlockSpec(shape, lambda *_: zeros, pipeline_mode=pl.Buffered(1))


def _param_specs(consts):
    specs, operands = [], []
    for c in consts:
        if isinstance(c, _LayerParam):
            tail = c.array.shape[1:]
            idx = (c.layer,) + (0,) * len(tail)
            specs.append(pl.BlockSpec((None,) + tail, lambda *_, idx=idx: idx, pipeline_mode=pl.Buffered(1)))
            operands.append(c.array)
        else:
            specs.append(_const_spec(c.shape))
            operands.append(c)
    return specs, operands


def _rope_kernel(pos_ref, freq_ref, cos_out, sin_out):
    ang = pos_ref[...].astype(F32) * freq_ref[...]
    cos_out[...] = jnp.cos(ang)
    sin_out[...] = jnp.sin(ang)


def _rope_tables(pos2, freq, tm):
    n = pos2.shape[0]
    row = lambda w: pl.BlockSpec((tm, w), lambda i: (i, 0))
    out = jax.ShapeDtypeStruct((n, LANES), F32)
    return pl.pallas_call(
        _rope_kernel,
        grid=(n // tm,),
        in_specs=[row(1), _const_spec(freq.shape)],
        out_specs=[row(LANES), row(LANES)],
        out_shape=[out, out],
        compiler_params=pltpu.CompilerParams(dimension_semantics=("parallel",), vmem_limit_bytes=VMEM_LIMIT),
        name="rope",
    )(pos2, freq)


def _front_kernel(x_ref, cos_ref, sin_ref, g_ref, wmix_ref, qn_ref, kvn_ref, wq_ref, wqr_ref, wk_ref, wvt_ref,
                  vones_ref, q_out, k_out, v_out, pb_out, pc_out, pd_lo_out, pd_hi_out):
    hn = _rms(x_ref[...], g_ref[...]).astype(BF16)
    pa = _dot(hn, wmix_ref[:, 0:W_SEC_A])
    pb_out[...] = _dot(hn, wmix_ref[:, W_SEC_A:W_SEC_A + W_SEC_B])
    pc_out[:, 0:W_SEC_C_MAIN] = _dot(hn, wmix_ref[:, W_SEC_A + W_SEC_B:W_SEC_A + W_SEC_B + W_SEC_C_MAIN])
    pc_out[:, W_SEC_C_MAIN:W_SEC_C] = pa[:, OFF_MLGATE:OFF_MLGATE + LANES]
    pd = _dot(hn, wmix_ref[:, W_SEC_A + W_SEC_B + W_SEC_C_MAIN:W_MIX])
    pd_lo_out[...] = pd[:, :LANES]
    pd_hi_out[...] = pd[:, LANES:]

    cos, sin = cos_ref[...], sin_ref[...]
    scale = (MLA_NOPE + MLA_ROPE) ** -0.5

    aqn = _rms(pa[:, OFF_AQ:OFF_AQ + MLA_Q_RANK], qn_ref[...]).astype(BF16)
    q0 = _dot(aqn, wq_ref[...])
    qr = _dot(aqn, wqr_ref[...])
    akvn = _rms(pa[:, OFF_AKV:OFF_AKV + MLA_KV_RANK], kvn_ref[...]).astype(BF16)
    kn = _dot(akvn, wk_ref[...])
    vt = _nt_dot(wvt_ref[...], akvn) + vones_ref[...]
    for t in range(v_out.shape[0]):
        v_out[t] = vt[:, t * ATT_BLK:(t + 1) * ATT_BLK].astype(v_out.dtype)
    k_rope = pa[:, OFF_KR:OFF_KR + LANES] * cos + pa[:, OFF_KRROT:OFF_KRROT + LANES] * sin
    for h in range(N_HEADS):
        sl = slice(h * LANES, (h + 1) * LANES)
        q_out[:, sl] = ((q0[:, sl] * cos + qr[:, sl] * sin) * scale).astype(q_out.dtype)
        k_out[:, sl] = (kn[:, sl] + k_rope).astype(k_out.dtype)


def _front(x2, rope_cos, rope_sin, lw, tm):
    n = x2.shape[0]
    row = lambda w: pl.BlockSpec((tm, w), lambda i: (i, 0))
    specs, consts = _param_specs([lw['g_mix_pre'], lw['w_mix'], lw['mla_q_norm'], lw['mla_kv_norm'], lw['wq'],
                                  lw['wqr'], lw['wk'], lw['wv_t'], lw['v_ones']])
    return pl.pallas_call(
        _front_kernel,
        grid=(n // tm,),
        in_specs=[row(D_MODEL), row(LANES), row(LANES)] + specs,
        out_specs=[row(4 * LANES), row(4 * LANES),
                   pl.BlockSpec((tm // ATT_BLK, 4 * LANES, ATT_BLK), lambda i: (i, 0, 0)), row(W_SEC_B),
                   row(W_SEC_C), row(LANES), row(LANES)],
        out_shape=[jax.ShapeDtypeStruct((n, 4 * LANES), BF16), jax.ShapeDtypeStruct((n, 4 * LANES), BF16),
                   jax.ShapeDtypeStruct((n // ATT_BLK, 4 * LANES, ATT_BLK), BF16),
                   jax.ShapeDtypeStruct((n, W_SEC_B), F32),
                   jax.ShapeDtypeStruct((n, W_SEC_C), F32), jax.ShapeDtypeStruct((n, LANES), F32),
                   jax.ShapeDtypeStruct((n, LANES), F32)],
        compiler_params=pltpu.CompilerParams(dimension_semantics=("parallel",), vmem_limit_bytes=VMEM_LIMIT),
        name="front",
    )(x2, rope_cos, rope_sin, *consts)


def _mla_kernel(q_ref, k_ref, vt_ref, o_ref, *, blk):
    i = pl.program_id(1)
    key = lax.broadcasted_iota(jnp.int32, (blk, blk), 0)
    qry = lax.broadcasted_iota(jnp.int32, (blk, blk), 1)
    causal = key <= qry

    chains = [(s, slice(h * LANES, (h + 1) * LANES)) for s in range(q_ref.shape[0]) for h in range(N_HEADS)]

    def step(j, carry, masked):
        start = pl.multiple_of(j * blk, blk)
        scores = [_nt_dot(k_ref[s, pl.ds(start, blk), sl], q_ref[s, :, sl]) for s, sl in chains]
        if masked:
            scores = [jnp.where(causal, sc, MASK_VALUE) for sc in scores]
        m_new = [jnp.maximum(c[0], jnp.max(sc, axis=0, keepdims=True)) for c, sc in zip(carry, scores)]
        probs = [jnp.exp(sc - m).astype(BF16) for sc, m in zip(scores, m_new)]
        pv = [_dot(vt_ref[s, j, sl, :], p) for (s, sl), p in zip(chains, probs)]
        return tuple((m, jnp.exp(c[0] - m) * c[1] + x) for c, m, x in zip(carry, m_new, pv))

    init = tuple((jnp.full((1, blk), MASK_VALUE, F32), jnp.zeros((LANES, blk), F32)) for _ in chains)
    carry = lax.fori_loop(0, i, functools.partial(step, masked=False), init)
    carry = step(i, carry, True)
    outs = [acc[:HEAD_W] * (1.0 / acc[HEAD_W:HEAD_W + 1]) for _, acc in carry]
    for s in range(q_ref.shape[0]):
        o_ref[s] = jnp.concatenate(outs[s * N_HEADS:(s + 1) * N_HEADS], axis=0).T.astype(o_ref.dtype)


def _mla(q, k, vt, batch, seq):
    blk = ATT_BLK
    q3, k3 = (a.reshape(batch, seq, 4 * LANES) for a in (q, k))
    vt4 = vt.reshape(batch, seq // blk, 4 * LANES, blk)
    nseq = ATT_SEQS if batch % ATT_SEQS == 0 else 1
    out = pl.pallas_call(
        functools.partial(_mla_kernel, blk=blk),
        grid=(batch // nseq, seq // blk),
        in_specs=[pl.BlockSpec((nseq, blk, 4 * LANES), lambda b, i: (b, i, 0)),
                  pl.BlockSpec((nseq, seq, 4 * LANES), lambda b, i: (b, 0, 0)),
                  pl.BlockSpec((nseq, seq // blk, 4 * LANES, blk), lambda b, i: (b, 0, 0, 0))],
        out_specs=pl.BlockSpec((nseq, blk, BRANCH_W), lambda b, i: (b, i, 0)),
        out_shape=jax.ShapeDtypeStruct((batch, seq, BRANCH_W), BF16),
        compiler_params=pltpu.CompilerParams(dimension_semantics=("parallel", "arbitrary"),
                                             vmem_limit_bytes=VMEM_LIMIT),
        name="mla",
    )(q3, k3, vt4)
    return out.reshape(batch * seq, BRANCH_W)


def _hgrn_kernel(pb_ref, lbl_ref, gain_ref, o_ref, st_ref, q_sc, k_sc, b_sc, lk_sc, o_sc, *, layer, tb):
    @pl.when(pl.program_id(1) == 0)
    def _():
        st_ref[...] = jnp.zeros_like(st_ref)

    lg = lbl_ref[...]
    e = jnp.exp(lg - jnp.max(lg, axis=0, keepdims=True))
    sm = e / jnp.sum(e, axis=0, keepdims=True)
    lb = jnp.zeros((1, BRANCH_W), F32)
    for r in range(1, layer + 1):
        lb = lb + sm[r:r + 1, :]
    one_m_lb = 1.0 - lb

    ones_bd = _head_block_ones(BF16)
    ones_bd_f32 = _head_block_ones(F32)
    row_in_sub = lax.broadcasted_iota(jnp.int32, (HG_CHUNK, BRANCH_W), 0) % SUB

    f_logit = pb_ref[:, BRANCH_W:2 * BRANCH_W]
    e = jnp.exp(-jnp.abs(f_logit))
    r = 1.0 / (1.0 + e)
    nonneg = f_logit >= 0.0
    forget = lb + one_m_lb * jnp.where(nonneg, r, e * r)
    b_all = _cumsum_rows(jnp.log(jnp.maximum(forget, GATE_FLOOR)), _chunk_lower_tri(tb, HG_CHUNK, BF16))
    b_sc[...] = b_all
    k_sc[...] = one_m_lb * jnp.where(nonneg, e * r, r)
    lk_sc[...] = jnp.log(one_m_lb) - jnp.maximum(f_logit, 0.0) + jnp.log(r) - b_all
    q_sc[...] = _silu(pb_ref[:, 0:BRANCH_W])

    def chunk_body(c, carry):
        start = pl.multiple_of(c * HG_CHUNK, HG_CHUNK)
        rows = pl.ds(start, HG_CHUNK)
        q, k, b = q_sc[rows, :], k_sc[rows, :], b_sc[rows, :]
        v = pb_ref[rows, 2 * BRANCH_W:3 * BRANCH_W]
        b_end = b[HG_CHUNK - 1:HG_CHUNK, :]

        st = st_ref[...]
        o = _nt_dot((q * jnp.exp(b)).astype(BF16), st.astype(BF16))
        k_end = k * jnp.exp(b_end - b)
        st_ref[...] = (st * jnp.exp(b_end) + _tn_dot(v.astype(BF16), k_end.astype(BF16))) * ones_bd_f32

        m = SUB
        while m < HG_CHUNK:
            mask = _head_stack_mask(m, F32)
            parts = []
            for lo in range(0, HG_CHUNK, 2 * m):
                left = slice(lo, lo + m)
                right = slice(lo + m, lo + 2 * m)
                ref = b[lo + m - 1:lo + m, :]
                kt = k[left] * jnp.exp(ref - b[left])
                qt = q[right] * jnp.exp(b[right] - ref)
                k_hat = (jnp.concatenate([kt] * N_HEADS, axis=0) * mask).astype(BF16)
                v_hat = (jnp.concatenate([v[left]] * N_HEADS, axis=0) * mask).astype(BF16)
                a = _nt_dot(qt.astype(BF16), k_hat)
                parts.append(jnp.zeros((m, BRANCH_W), F32))
                parts.append(_dot(a.astype(BF16), v_hat))
            o = o + jnp.concatenate(parts, axis=0)
            m *= 2

        def key_row(ref, col, s):
            return jnp.concatenate(
                [jnp.broadcast_to(ref[pl.ds(start + blk * SUB + s, 1), col:col + BRANCH_W], (SUB, BRANCH_W))
                 for blk in range(HG_CHUNK // SUB)], axis=0)

        for s in range(SUB):
            expo = b + key_row(lk_sc, 0, s)
            if s > 0:
                expo = jnp.where(row_in_sub >= s, expo, MASK_VALUE)
            score = _dot((q * jnp.exp(expo)).astype(BF16), ones_bd)
            o = o + score * key_row(pb_ref, 2 * BRANCH_W, s)
        o_sc[rows, :] = o
        return carry

    lax.fori_loop(0, tb // HG_CHUNK, chunk_body, 0, unroll=True)

    gate = _silu(pb_ref[:, 3 * BRANCH_W:4 * BRANCH_W])
    o_ref[...] = (_head_norm(o_sc[...], gain_ref[...], ones_bd) * gate).astype(o_ref.dtype)


def _hgrn(pb, lb_logits, gain, layer, batch, seq, tb):
    pb3 = pb.reshape(batch, seq, W_SEC_B)
    specs, consts = _param_specs([lb_logits, gain])
    blk_scratch = pltpu.VMEM((tb, BRANCH_W), F32)
    out = pl.pallas_call(
        functools.partial(_hgrn_kernel, layer=layer, tb=tb),
        grid=(batch, seq // tb),
        in_specs=[pl.BlockSpec((None, tb, W_SEC_B), lambda b, i: (b, i, 0))] + specs,
        out_specs=pl.BlockSpec((None, tb, BRANCH_W), lambda b, i: (b, i, 0)),
        out_shape=jax.ShapeDtypeStruct((batch, seq, BRANCH_W), BF16),
        scratch_shapes=[pltpu.VMEM((BRANCH_W, BRANCH_W), F32)] + [blk_scratch] * 5,
        compiler_params=pltpu.CompilerParams(dimension_semantics=("parallel", "arbitrary"),
                                             vmem_limit_bytes=VMEM_LIMIT),
        name="hgrn",
    )(pb3, *consts)
    return out.reshape(batch * seq, BRANCH_W)


def _mlstm_kernel(pc_ref, cw_ref, cb_ref, wq_ref, wk_ref, ib_ref, fb_ref, gain_ref, o_ref,
                  xext_ref, q_sc, k_sc, il_sc, b_sc, h_sc, c_ref, n_ref, m_ref, *, tb):
    pad = 8
    seqs = range(pc_ref.shape[0])

    @pl.when(pl.program_id(1) == 0)
    def _():
        xext_ref[:, 0:pad, :] = jnp.zeros((len(seqs), pad, BRANCH_W), F32)
        c_ref[...] = jnp.zeros_like(c_ref)
        n_ref[...] = jnp.zeros_like(n_ref)
        m_ref[...] = jnp.zeros_like(m_ref)

    gr = lax.broadcasted_iota(jnp.int32, (LANES, BRANCH_W), 0)
    gc = lax.broadcasted_iota(jnp.int32, (LANES, BRANCH_W), 1) // HEAD_W
    tri = _chunk_lower_tri(tb, CHUNK, BF16)
    for s in seqs:
        xext_ref[s, pad:pad + tb, :] = pc_ref[s, :, 0:BRANCH_W]
        conv = jnp.zeros((tb, BRANCH_W), F32) + cb_ref[...]
        for j in range(ML_CONV):
            first = pad - (ML_CONV - 1) + j
            conv = conv + xext_ref[s, first:first + tb, :] * cw_ref[j:j + 1, :]
        xext_ref[s, 0:pad, :] = xext_ref[s, tb:tb + pad, :]
        xc = _silu(conv).astype(BF16)
        q_sc[s] = _dot(xc, wq_ref[...])
        k_sc[s] = _dot(xc, wk_ref[...]) * (HEAD_W ** -0.5)
        gates = pc_ref[s, :, 3 * BRANCH_W:3 * BRANCH_W + LANES]
        il_sc[s] = _dot_split(gates, (gr == gc).astype(BF16)) + ib_ref[...]
        f_log = _log_sigmoid(_dot_split(gates, (gr == gc + N_HEADS).astype(BF16)) + fb_ref[...])
        b_sc[s] = _cumsum_rows(f_log, tri)

    ones_bd = _head_block_ones(BF16)
    ones_bd_f32 = _head_block_ones(F32)
    stack_mask = _head_stack_mask(CHUNK, F32)
    lane = lax.broadcasted_iota(jnp.int32, (CHUNK, BRANCH_W), 1)
    rowi = lax.broadcasted_iota(jnp.int32, (CHUNK, BRANCH_W), 0)
    diag_sel = (lane % HEAD_W == rowi).astype(F32)
    causal = (lane % HEAD_W) <= rowi

    def head_max(a):
        out = jnp.zeros_like(a)
        for h in range(N_HEADS):
            mx = jnp.max(a[:, h * HEAD_W:(h + 1) * HEAD_W], axis=-1, keepdims=True)
            out = jnp.where(lane // HEAD_W == h, mx, out)
        return out

    def chunk_body(c, carry):
        rows = pl.ds(pl.multiple_of(c * CHUNK, CHUNK), CHUNK)
        q = [q_sc[s, rows, :] for s in seqs]
        k = [k_sc[s, rows, :] for s in seqs]
        i_log = [il_sc[s, rows, :] for s in seqs]
        b = [b_sc[s, rows, :] for s in seqs]
        v = [pc_ref[s, rows, BRANCH_W:2 * BRANCH_W] for s in seqs]
        qb = [a.astype(BF16) for a in q]
        k_hat = [(jnp.concatenate([a] * N_HEADS, axis=0) * stack_mask).astype(BF16) for a in k]
        v_hat = [(jnp.concatenate([a] * N_HEADS, axis=0) * stack_mask).astype(BF16) for a in v]
        scores = [_nt_dot(qb[s], k_hat[s]) for s in seqs]
        c0, n0, m0 = [c_ref[s] for s in seqs], [n_ref[s] for s in seqs], [m_ref[s] for s in seqs]
        inter_num = [_dot(qb[s], c0[s].astype(BF16)) for s in seqs]
        inter_den = [_dot((q[s] * n0[s]).astype(BF16), ones_bd) for s in seqs]

        b_end = [a[CHUNK - 1:CHUNK, :] for a in b]
        g_end = [b_end[s] - b[s] + i_log[s] for s in seqs]
        m_new = [jnp.maximum(b_end[s] + m0[s], jnp.max(g_end[s], axis=0, keepdims=True)) for s in seqs]
        kw = [k[s] * jnp.exp(g_end[s] - m_new[s]) for s in seqs]
        kv = [_tn_dot(kw[s].astype(BF16), v[s].astype(BF16)) for s in seqs]

        b_row = [jnp.sum(a * diag_sel, axis=0, keepdims=True) for a in b]
        i_row = [jnp.sum(a * diag_sel, axis=0, keepdims=True) for a in i_log]
        dmat = [jnp.where(causal, b[s] - b_row[s] + i_row[s], MASK_VALUE) for s in seqs]
        a_t = [b[s] + m0[s] for s in seqs]
        m_t = [jnp.maximum(a_t[s], head_max(dmat[s])) for s in seqs]
        w_inter = [jnp.exp(a_t[s] - m_t[s]) for s in seqs]
        qk = [(scores[s] * jnp.exp(dmat[s] - m_t[s])).astype(BF16) for s in seqs]
        num = [_dot(qk[s], v_hat[s]) for s in seqs]
        den = [_dot(qk[s], ones_bd) for s in seqs]
        for s in seqs:
            h_num = num[s] + w_inter[s] * inter_num[s]
            h_den = den[s] + w_inter[s] * inter_den[s]
            h_sc[s, rows, :] = h_num / jnp.maximum(jnp.abs(h_den), jnp.exp(-m_t[s]))
            keep = jnp.exp(b_end[s] + m0[s] - m_new[s])
            c_ref[s] = (keep * c0[s] + kv[s]) * ones_bd_f32
            n_ref[s] = keep * n0[s] + jnp.sum(kw[s], axis=0, keepdims=True)
            m_ref[s] = m_new[s]
        return carry

    lax.fori_loop(0, tb // CHUNK, chunk_body, 0, unroll=2)

    for s in seqs:
        out_gate = _sigmoid(pc_ref[s, :, 2 * BRANCH_W:3 * BRANCH_W])
        o_ref[s] = (out_gate * _head_norm(h_sc[s], gain_ref[...], ones_bd)).astype(o_ref.dtype)


def _mlstm(pc, lw, batch, seq, tb):
    pc3 = pc.reshape(batch, seq, W_SEC_C)
    specs, consts = _param_specs([lw['ml_conv_w'], lw['ml_conv_b'], lw['ml_wq_bd'], lw['ml_wk_bd'],
                                  lw['ml_i_bias'], lw['ml_f_bias'], lw['ml_out_norm']])
    nseq = MIX_SEQS if batch % MIX_SEQS == 0 else 1
    blk_scratch = pltpu.VMEM((nseq, tb, BRANCH_W), F32)
    out = pl.pallas_call(
        functools.partial(_mlstm_kernel, tb=tb),
        grid=(batch // nseq, seq // tb),
        in_specs=[pl.BlockSpec((nseq, tb, W_SEC_C), lambda b, i: (b, i, 0))] + specs,
        out_specs=pl.BlockSpec((nseq, tb, BRANCH_W), lambda b, i: (b, i, 0)),
        out_shape=jax.ShapeDtypeStruct((batch, seq, BRANCH_W), BF16),
        scratch_shapes=[pltpu.VMEM((nseq, tb + 8, BRANCH_W), F32)] + [blk_scratch] * 5
                       + [pltpu.VMEM((nseq, BRANCH_W, BRANCH_W), F32), pltpu.VMEM((nseq, 1, BRANCH_W), F32),
                          pltpu.VMEM((nseq, 1, BRANCH_W), F32)],
        compiler_params=pltpu.CompilerParams(dimension_semantics=("parallel", "arbitrary"),
                                             vmem_limit_bytes=VMEM_LIMIT),
        name="mlstm",
    )(pc3, *consts)
    return out.reshape(batch * seq, BRANCH_W)


def _lane_block_transpose(rows):
    n = len(rows)
    width = rows[0].shape[1]
    blk = lax.broadcasted_iota(jnp.int32, rows[0].shape, 1) // (width // n)
    d = n // 2
    while d >= 1:
        upper = (blk & d) != 0
        shift = d * (width // n)
        nxt = list(rows)
        for i in range(n):
            if i & d == 0:
                lo, hi = rows[i], rows[i + d]
                nxt[i] = jnp.where(upper, pltpu.roll(hi, shift, 1), lo)
                nxt[i + d] = jnp.where(upper, hi, pltpu.roll(lo, width - shift, 1))
        rows = nxt
        d //= 2
    return rows


def _s5_kernel(u_lo_ref, u_hi_ref, tz_ref, min_re_ref, min_im_ref, mout_re_ref, mout_im_ref, pw_re_ref,
               pw_im_ref, d_ref, y_lo_ref, y_hi_ref, ug_ref, yg_ref, *, n_rows):
    nb = LANES // S5_GROUP_CH
    for tok_half in range(S5_CHUNK // nb):
        for grp_half, u_ref in enumerate((u_lo_ref, u_hi_ref)):
            by_token = [u_ref[pl.ds(nb * tok_half + t, n_rows, stride=S5_CHUNK), :] for t in range(nb)]
            by_group = _lane_block_transpose(by_token)
            for g in range(nb):
                ug_ref[nb * grp_half + g, :, tok_half * LANES:(tok_half + 1) * LANES] = by_group[g]

    row = lax.broadcasted_iota(jnp.int32, (n_rows, LANES), 0)
    for pair in range(S5_GROUPS // 2):
        u = [ug_ref[2 * pair + s] for s in range(2)]
        ub = [a.astype(BF16) for a in u]
        s_re = _dot(ub[0], min_re_ref[2 * pair]) + _dot(ub[1], min_re_ref[2 * pair + 1])
        s_im = _dot(ub[0], min_im_ref[2 * pair]) + _dot(ub[1], min_im_ref[2 * pair + 1])
        step, k = 1, 0
        while step < n_rows:
            keep = row >= step
            p_re = jnp.where(keep, pltpu.roll(s_re, step, 0), 0.0)
            p_im = jnp.where(keep, pltpu.roll(s_im, step, 0), 0.0)
            a_re = pw_re_ref[pair, k:k + 1, :]
            a_im = pw_im_ref[pair, k:k + 1, :]
            s_re, s_im = s_re + a_re * p_re - a_im * p_im, s_im + a_re * p_im + a_im * p_re
            step, k = step * 2, k + 1
        first = row >= 1
        s0_re = jnp.where(first, pltpu.roll(s_re, 1, 0), 0.0).astype(BF16)
        s0_im = jnp.where(first, pltpu.roll(s_im, 1, 0), 0.0).astype(BF16)
        for s in range(2):
            g = 2 * pair + s
            yg_ref[g] = (_dot(ub[s], tz_ref[g]) + _dot(s0_re, mout_re_ref[g]) + _dot(s0_im, mout_im_ref[g])
                         + u[s] * d_ref[g])

    for tok_half in range(S5_CHUNK // nb):
        for grp_half, y_ref in enumerate((y_lo_ref, y_hi_ref)):
            by_group = [yg_ref[nb * grp_half + g, :, tok_half * LANES:(tok_half + 1) * LANES] for g in range(nb)]
            by_token = _lane_block_transpose(by_group)
            for t in range(nb):
                y_ref[pl.ds(nb * tok_half + t, n_rows, stride=S5_CHUNK), :] = by_token[t]


def _s5(pd_lo, pd_hi, lw, batch, seq):
    n_rows = seq // S5_CHUNK
    assert n_rows <= 2 ** S5_SCAN_STEPS
    width = S5_CHUNK * S5_GROUP_CH
    specs, consts = _param_specs([lw['s5_tz'], lw['s5_min_re'], lw['s5_min_im'], lw['s5_mout_re'],
                                  lw['s5_mout_im'], lw['s5_pw_re'], lw['s5_pw_im'], lw['s5_d']])
    seq_blk = pl.BlockSpec((None, seq, LANES), lambda b: (b, 0, 0))
    out = jax.ShapeDtypeStruct((batch, seq, LANES), F32)
    y_lo, y_hi = pl.pallas_call(
        functools.partial(_s5_kernel, n_rows=n_rows),
        grid=(batch,),
        in_specs=[seq_blk, seq_blk] + specs,
        out_specs=[seq_blk, seq_blk],
        out_shape=[out, out],
        scratch_shapes=[pltpu.VMEM((S5_GROUPS, n_rows, width), F32), pltpu.VMEM((S5_CHUNK, n_rows, BRANCH_W), F32)],
        compiler_params=pltpu.CompilerParams(dimension_semantics=("parallel",), vmem_limit_bytes=VMEM_LIMIT),
        name="s5",
    )(pd_lo.reshape(batch, seq, LANES), pd_hi.reshape(batch, seq, LANES), *consts)
    return y_lo.reshape(batch * seq, LANES), y_hi.reshape(batch * seq, LANES)


def _merge_kernel(x_ref, a_ref, b_ref, c_ref, y_lo_ref, y_hi_ref, gpre_ref, wgate_ref, wbr_ref, wout_ref,
                  wglu_ref, bglu_ref, gpost_ref, o_ref):
    x = x_ref[...]
    hn = _rms(x, gpre_ref[...]).astype(BF16)
    y = _gelu_tanh(jnp.concatenate([y_lo_ref[...], y_hi_ref[...]], axis=-1))
    out_d = (y * _sigmoid(_dot(y.astype(BF16), wglu_ref[...]) + bglu_ref[...])).astype(BF16)
    branches = (a_ref[...], b_ref[...], c_ref[...], out_d)
    merged = None
    for n in range(4):
        gate = _sigmoid(_dot(hn, wgate_ref[:, n * D_MODEL:(n + 1) * D_MODEL]))
        term = gate * _dot(branches[n], wbr_ref[n])
        merged = term if merged is None else merged + term
    mix = _dot(merged.astype(BF16), wout_ref[...])
    o_ref[...] = x + _rms(mix, gpost_ref[...])


def _merge(x2, out_a, out_b, out_c, y_lo, y_hi, lw, tm):
    n = x2.shape[0]
    row = lambda w: pl.BlockSpec((tm, w), lambda i: (i, 0))
    specs, consts = _param_specs([lw['g_mix_pre'], lw['w_gate'], lw['w_branch'], lw['w_out'], lw['s5_w_glu'],
                                  lw['s5_b_glu'], lw['g_mix_post']])
    return pl.pallas_call(
        _merge_kernel,
        grid=(n // tm,),
        in_specs=[row(D_MODEL), row(BRANCH_W), row(BRANCH_W), row(BRANCH_W), row(LANES), row(LANES)] + specs,
        out_specs=row(D_MODEL),
        out_shape=jax.ShapeDtypeStruct((n, D_MODEL), F32),
        compiler_params=pltpu.CompilerParams(dimension_semantics=("parallel",), vmem_limit_bytes=VMEM_LIMIT),
        name="merge",
    )(x2, out_a, out_b, out_c, y_lo, y_hi, *consts)


def _memkv_kernel(mem_ref, g_ref, wk_ref, wv_ref, k_out, v_out):
    mn = _rms(mem_ref[...], g_ref[...]).astype(BF16)
    k_out[...] = _dot(mn, wk_ref[...].astype(BF16)).astype(k_out.dtype)
    v_out[...] = _dot(mn, wv_ref[...].astype(BF16)).astype(v_out.dtype)


def _memkv(mem, lw):
    batch, mlen, _ = mem.shape
    specs, consts = _param_specs([lw['g_mem'], lw['xa_wk'], lw['xa_wv']])
    blk = pl.BlockSpec((None, mlen, D_MODEL), lambda b: (b, 0, 0))
    return pl.pallas_call(
        _memkv_kernel,
        grid=(batch,),
        in_specs=[blk] + specs,
        out_specs=[blk, blk],
        out_shape=[jax.ShapeDtypeStruct((batch, mlen, D_MODEL), BF16)] * 2,
        compiler_params=pltpu.CompilerParams(dimension_semantics=("parallel",), vmem_limit_bytes=VMEM_LIMIT),
        name="memkv",
    )(mem, *consts)


def _xattn_kernel(x_ref, k_ref, v_ref, gpre_ref, wq_ref, wo_ref, gpost_ref, o_ref):
    x = x_ref[...]
    hn = _rms(x, gpre_ref[...]).astype(BF16)
    q = (_dot(hn, wq_ref[...].astype(BF16)) * (XA_DIM ** -0.5)).astype(BF16)
    outs = []
    for h in range(XA_HEADS):
        sl = slice(h * XA_DIM, (h + 1) * XA_DIM)
        s = _nt_dot(q[:, sl], k_ref[:, sl])
        p = jnp.exp(s - jnp.max(s, axis=-1, keepdims=True))
        inv = 1.0 / jnp.sum(p, axis=-1, keepdims=True)
        outs.append((_dot(p.astype(BF16), v_ref[:, sl]) * inv).astype(BF16))
    xa = _dot(jnp.concatenate(outs, axis=-1), wo_ref[...].astype(BF16))
    o_ref[...] = x + _rms(xa, gpost_ref[...])


def _xattn(x2, mem_k, mem_v, lw, batch, seq, tm):
    mlen = mem_k.shape[1]
    x3 = x2.reshape(batch, seq, D_MODEL)
    specs, consts = _param_specs([lw['g_xa_pre'], lw['xa_wq'], lw['xa_wo'], lw['g_xa_post']])
    row = pl.BlockSpec((None, tm, D_MODEL), lambda b, i: (b, i, 0))
    kv = pl.BlockSpec((None, mlen, D_MODEL), lambda b, i: (b, 0, 0))
    out = pl.pallas_call(
        _xattn_kernel,
        grid=(batch, seq // tm),
        in_specs=[row, kv, kv] + specs,
        out_specs=row,
        out_shape=jax.ShapeDtypeStruct((batch, seq, D_MODEL), F32),
        compiler_params=pltpu.CompilerParams(dimension_semantics=("parallel", "parallel"),
                                             vmem_limit_bytes=VMEM_LIMIT),
        name="xattn",
    )(x3, mem_k, mem_v, *consts)
    return out.reshape(batch * seq, D_MODEL)


def _ffn_kernel(x_ref, gpre_ref, win_ref, wo_ref, gpost_ref, o_ref):
    x = x_ref[...]
    hn = _rms(x, gpre_ref[...]).astype(BF16)
    acc = None
    for j in range(D_FF // FF_CHUNK):
        lo = j * FF_CHUNK
        a = _dot(hn, win_ref[:, lo:lo + FF_CHUNK])
        b = _dot(hn, win_ref[:, D_FF + lo:D_FF + lo + FF_CHUNK])
        part = _dot((_silu(a) * b).astype(BF16), wo_ref[lo:lo + FF_CHUNK, :])
        acc = part if acc is None else acc + part
    o_ref[...] = x + _rms(acc, gpost_ref[...])


def _ffn(x2, lw, tm):
    n = x2.shape[0]
    row = pl.BlockSpec((tm, D_MODEL), lambda i: (i, 0))
    specs, consts = _param_specs([lw['g_ffn_pre'], lw['ffn_w_in'], lw['ffn_wo'], lw['g_ffn_post']])
    return pl.pallas_call(
        _ffn_kernel,
        grid=(n // tm,),
        in_specs=[row] + specs,
        out_specs=row,
        out_shape=jax.ShapeDtypeStruct((n, D_MODEL), F32),
        compiler_params=pltpu.CompilerParams(dimension_semantics=("parallel",), vmem_limit_bytes=VMEM_LIMIT),
        name="ffn",
    )(x2, *consts)


def _rot_half(w):
    half = MLA_ROPE // 2
    return jnp.concatenate([-w[..., half:], w[..., :half]], axis=-1)


def _block_diag(w):
    depth, h, d, e = w.shape
    eye = jnp.eye(h, dtype=w.dtype)
    return (eye[None, :, None, :, None] * w[:, :, :, None, :]).reshape(depth, h * d, h * e)


def _s5_tables(a_re, a_im, log_dt, b_re, b_im, c_re, c_im, d):
    hi = lax.Precision.HIGHEST
    g, p, hch = b_re.shape
    L = S5_CHUNK
    width = L * hch
    dt = jnp.exp(log_dt)[:, None]
    lam_re, lam_im = a_re * dt, a_im * dt
    mag = jnp.exp(lam_re)
    ab_re, ab_im = mag * jnp.cos(lam_im), mag * jnp.sin(lam_im)
    inv_abs2 = 1.0 / (a_re * a_re + a_im * a_im)
    z_re = ((ab_re - 1.0) * a_re + ab_im * a_im) * inv_abs2
    z_im = (ab_im * a_re - (ab_re - 1.0) * a_im) * inv_abs2
    bt_re, bt_im = b_re.swapaxes(1, 2), b_im.swapaxes(1, 2)
    bbt_re = z_re[:, None, :] * bt_re - z_im[:, None, :] * bt_im
    bbt_im = z_re[:, None, :] * bt_im + z_im[:, None, :] * bt_re

    def power(k):
        k = jnp.asarray(k, F32)[None, :, None]
        m = jnp.exp(k * lam_re[:, None, :])
        return m * jnp.cos(k * lam_im[:, None, :]), m * jnp.sin(k * lam_im[:, None, :])

    pw_re, pw_im = power(jnp.arange(L + 1))
    ct_re, ct_im = c_re.swapaxes(1, 2)[:, :, None, :], c_im.swapaxes(1, 2)[:, :, None, :]
    pk_re, pk_im = pw_re.swapaxes(1, 2)[:, :, :, None], pw_im.swapaxes(1, 2)[:, :, :, None]
    cp_re = (ct_re * pk_re - ct_im * pk_im).reshape(g, p, (L + 1) * hch)
    cp_im = (ct_re * pk_im + ct_im * pk_re).reshape(g, p, (L + 1) * hch)
    kern = (jnp.einsum('ghp,gpx->ghx', bbt_re, cp_re, precision=hi)
            - jnp.einsum('ghp,gpx->ghx', bbt_im, cp_im, precision=hi))
    x = jnp.arange((L + 1) * hch)
    y = jnp.arange(width)
    lag = y[None, None, :] // hch - jnp.arange(L)[:, None, None]
    place = ((x[None, :, None] // hch == lag) & (x[None, :, None] % hch == y[None, None, :] % hch)).astype(BF16)
    tz = jnp.einsum('ghx,lxy->glhy', kern.astype(BF16), place, preferred_element_type=F32)
    tz = tz.astype(BF16).reshape(g, width, width)

    rp_re, rp_im = pw_re[:, L - 1 - jnp.arange(L), None, :], pw_im[:, L - 1 - jnp.arange(L), None, :]
    min_re = (rp_re * bbt_re[:, None] - rp_im * bbt_im[:, None]).reshape(g, width, p)
    min_im = (rp_re * bbt_im[:, None] + rp_im * bbt_re[:, None]).reshape(g, width, p)
    mout_re, mout_im = cp_re[:, :, hch:], -cp_im[:, :, hch:]

    even_group = (jnp.arange(g) % 2 == 0)[:, None, None]

    def side_by_side(a, axis):
        z = jnp.zeros_like(a)
        return jnp.where(even_group, jnp.concatenate([a, z], axis=axis), jnp.concatenate([z, a], axis=axis))

    st_re, st_im = power(L * (2 ** jnp.arange(S5_SCAN_STEPS)))
    pair_lanes = lambda a: a.reshape(g // 2, 2, S5_SCAN_STEPS, p).swapaxes(1, 2).reshape(g // 2, S5_SCAN_STEPS, 2 * p)
    return dict(
        s5_tz=tz,
        s5_min_re=side_by_side(min_re, 2).astype(BF16), s5_min_im=side_by_side(min_im, 2).astype(BF16),
        s5_mout_re=side_by_side(mout_re, 1).astype(BF16), s5_mout_im=side_by_side(mout_im, 1).astype(BF16),
        s5_pw_re=pair_lanes(st_re), s5_pw_im=pair_lanes(st_im),
        s5_d=jnp.tile(d, (1, L)).reshape(g, 1, width),
    )


def _stacked_weights(p):
    depth = p['w_in'].shape[0]
    cols = lambda lo, hi: p['w_in'][:, :, lo:hi].astype(BF16)
    zeros = lambda n: jnp.zeros((depth, D_MODEL, n), BF16)
    kr = cols(IN_KR, IN_B)
    rope_pad = LANES - MLA_NOPE - MLA_ROPE
    w_mix = jnp.concatenate([
        cols(0, IN_KR),
        zeros(MLA_NOPE), kr, zeros(rope_pad),
        zeros(MLA_NOPE), _rot_half(kr), zeros(rope_pad),
        cols(IN_CG, IN_D), zeros(LANES - 2 * N_HEADS),
        cols(IN_B, IN_C),
        cols(IN_C, IN_CG),
        cols(IN_D, IN_GATE)], axis=2)
    assert w_mix.shape[2] == W_MIX

    uq = p['mla_w_uq'].astype(BF16).reshape(depth, MLA_Q_RANK, N_HEADS, MLA_NOPE + MLA_ROPE)
    zq = jnp.zeros((depth, MLA_Q_RANK, N_HEADS, rope_pad), BF16)
    wq = jnp.concatenate([uq, zq], axis=-1).reshape(depth, MLA_Q_RANK, N_HEADS * LANES)
    wqr = jnp.concatenate([jnp.zeros((depth, MLA_Q_RANK, N_HEADS, MLA_NOPE), BF16),
                           _rot_half(uq[..., MLA_NOPE:]), zq], axis=-1).reshape(depth, MLA_Q_RANK, N_HEADS * LANES)
    ukv = p['mla_w_ukv'].astype(BF16).reshape(depth, MLA_KV_RANK, N_HEADS, MLA_NOPE + HEAD_W)
    zk = jnp.zeros((depth, MLA_KV_RANK, N_HEADS, LANES - MLA_NOPE), BF16)
    wk = jnp.concatenate([ukv[..., :MLA_NOPE], zk], axis=-1).reshape(depth, MLA_KV_RANK, N_HEADS * LANES)
    wv_t = jnp.concatenate([ukv[..., MLA_NOPE:], zk], axis=-1).reshape(depth, MLA_KV_RANK, N_HEADS * LANES)
    wv_t = wv_t.swapaxes(1, 2)

    spread = lambda b: jnp.repeat(b, HEAD_W, axis=-1)[:, None, :]
    row = lambda a: a[:, None, :]
    sw = dict(
        g_mix_pre=row(p['norm_mix_pre']), g_mix_post=row(p['norm_mix_post']),
        w_mix=w_mix, w_gate=cols(IN_GATE, p['w_in'].shape[2]),
        mla_q_norm=row(p['mla_q_norm']), mla_kv_norm=row(p['mla_kv_norm']),
        wq=wq, wqr=wqr, wk=wk, wv_t=wv_t,
        hg_out_norm=row(p['hg_out_norm']),
        ml_conv_w=p['ml_conv_w'], ml_conv_b=row(p['ml_conv_b']),
        ml_wq_bd=_block_diag(p['ml_w_q'].astype(BF16)), ml_wk_bd=_block_diag(p['ml_w_k'].astype(BF16)),
        ml_i_bias=spread(p['ml_i_bias']), ml_f_bias=spread(p['ml_f_bias']),
        ml_out_norm=row(p['ml_out_norm']),
        s5_w_glu=p['s5_w_glu'].astype(BF16), s5_b_glu=row(p['s5_b_glu']),
        w_branch=p['w_branch'].astype(BF16), w_out=p['w_out'].astype(BF16),
        g_xa_pre=row(p['norm_xa_pre']), g_xa_post=row(p['norm_xa_post']), g_mem=row(p['norm_mem']),
        xa_wq=p['xa_wq'], xa_wk=p['xa_wk'], xa_wv=p['xa_wv'], xa_wo=p['xa_wo'],
        g_ffn_pre=row(p['norm_ffn_pre']), g_ffn_post=row(p['norm_ffn_post']),
        ffn_w_in=p['ffn_w_in'].astype(BF16), ffn_wo=p['ffn_w_out'].astype(BF16),
    )
    sw.update(jax.vmap(_s5_tables)(p['s5_a_re'], p['s5_a_im'], p['s5_log_dt'], p['s5_b_re'], p['s5_b_im'],
                                   p['s5_c_re'], p['s5_c_im'], p['s5_d']))
    return sw


def _shared_constants():
    half = MLA_ROPE // 2
    inv_freq = ROPE_THETA ** (-np.arange(half, dtype=np.float32) / half)
    freq = np.zeros((1, LANES), np.float32)
    freq[0, MLA_NOPE:MLA_NOPE + half] = inv_freq
    freq[0, MLA_NOPE + half:MLA_NOPE + MLA_ROPE] = inv_freq
    v_ones = np.zeros((N_HEADS * LANES, 1), np.float32)
    v_ones[HEAD_W::LANES] = 1.0
    return dict(freq=jnp.asarray(freq), v_ones=jnp.asarray(v_ones))


def kernel(x, mem, positions, norm_mix_pre, norm_mix_post, w_in, mla_q_norm, mla_w_uq, mla_kv_norm, mla_w_ukv, hg_lb_logits, hg_out_norm, ml_conv_w, ml_conv_b, ml_w_q, ml_w_k, ml_i_bias, ml_f_bias, ml_out_norm, s5_a_re, s5_a_im, s5_log_dt, s5_b_re, s5_b_im, s5_c_re, s5_c_im, s5_d, s5_w_glu, s5_b_glu, w_branch, w_out, norm_xa_pre, norm_xa_post, norm_mem, xa_wq, xa_wk, xa_wv, xa_wo, norm_ffn_pre, norm_ffn_post, ffn_w_in, ffn_w_out):
    p = dict(locals())
    batch, seq, _ = x.shape
    depth = w_in.shape[0]
    n = batch * seq
    tm = min(ROW_TILE, seq)
    tb = min(SEQ_TILE, seq)
    x2 = x.reshape(n, D_MODEL)
    pos2 = positions.reshape(n, 1).astype(jnp.int32)
    stacked = _stacked_weights(p)
    shared = _shared_constants()
    rope_cos, rope_sin = _rope_tables(pos2, shared['freq'], tm)
    for l in range(depth):
        lw = {name: _LayerParam(a, l) for name, a in stacked.items()}
        lw.update(shared)
        q, k, vt, pb, pc, pd_lo, pd_hi = _front(x2, rope_cos, rope_sin, lw, tm)
        out_a = _mla(q, k, vt, batch, seq)
        out_b = _hgrn(pb, hg_lb_logits, lw['hg_out_norm'], l, batch, seq, tb)
        out_c = _mlstm(pc, lw, batch, seq, tb)
        y_lo, y_hi = _s5(pd_lo, pd_hi, lw, batch, seq)
        x2 = _merge(x2, out_a, out_b, out_c, y_lo, y_hi, lw, tm)
        mem_k, mem_v = _memkv(mem, lw)
        x2 = _xattn(x2, mem_k, mem_v, lw, batch, seq, tm)
        x2 = _ffn(x2, lw, tm)
    return x2.reshape(batch, seq, D_MODEL)
```

```python
import functools
import math
from typing import NamedTuple

import numpy as np
import jax
import jax.numpy as jnp
from jax import lax
from jax.experimental import pallas as pl
from jax.experimental.pallas import tpu as pltpu

F32 = jnp.float32
BF16 = jnp.bfloat16

D_MODEL = 1024
NORM_EPS = 1e-6
MASK_VALUE = -1e30
GATE_FLOOR = 1e-30
BRANCH_W = 256
N_HEADS = 4
HEAD_W = 64
MLA_Q_RANK = 256
MLA_KV_RANK = 128
MLA_NOPE = 64
MLA_ROPE = 32
ROPE_THETA = 10000.0
ML_CONV = 4
S5_GROUPS = 16
S5_GROUP_CH = 16
S5_CHUNK = 16
S5_SCAN_STEPS = 8
XA_HEADS = 4
XA_DIM = 256
D_FF = 2816
FF_CHUNK = 256

LANES = 128
ATT_BLK = 256
MIX_SEQS = 2
ATT_SEQS = 4
CHUNK = 64
HG_CHUNK = 128
SUB = 16
ROW_TILE = 1024
SEQ_TILE = 512
VMEM_LIMIT = 56 * 1024 * 1024

OFF_AQ, OFF_AKV, OFF_KR, OFF_KRROT, OFF_MLGATE, W_SEC_A = 0, 256, 384, 512, 640, 768
W_SEC_B = 4 * BRANCH_W
W_SEC_C_MAIN = 3 * BRANCH_W
W_SEC_C = W_SEC_C_MAIN + LANES
W_SEC_D = BRANCH_W
W_MIX = W_SEC_A + W_SEC_B + W_SEC_C_MAIN + W_SEC_D
IN_KR, IN_B, IN_C, IN_CG, IN_D, IN_GATE = 384, 416, 1440, 2208, 2216, 2472


def _nt_dot(a, b):
    return lax.dot_general(a, b, (((1,), (1,)), ((), ())), preferred_element_type=F32)


def _tn_dot(a, b):
    return lax.dot_general(a, b, (((0,), (0,)), ((), ())), preferred_element_type=F32)


def _dot(a, b):
    return jnp.dot(a, b, preferred_element_type=F32)


def _dot_split(a, b_bf16, terms=3):
    acc = None
    rem = a
    for _ in range(terms):
        piece = rem.astype(BF16)
        part = _dot(piece, b_bf16)
        acc = part if acc is None else acc + part
        rem = rem - piece.astype(F32)
    return acc


def _cumsum_rows(x, tri):
    acc = None
    rem = x
    for _ in range(2):
        piece = rem.astype(BF16)
        part = _dot(tri, piece)
        acc = part if acc is None else acc + part
        rem = rem - piece.astype(F32)
    return acc


def _rms(x, gain):
    return x * lax.rsqrt(jnp.mean(x * x, axis=-1, keepdims=True) + NORM_EPS) * gain


def _sigmoid(x):
    return 0.5 + 0.5 * jnp.tanh(0.5 * x)


def _silu(x):
    return x * _sigmoid(x)


def _log_sigmoid(x):
    return jnp.minimum(x, 0.0) - jnp.log(1.0 + jnp.exp(-jnp.abs(x)))


def _gelu_tanh(x):
    c = math.sqrt(2.0 / math.pi)
    return 0.5 * x * (1.0 + jnp.tanh(c * (x + 0.044715 * (x * x * x))))


def _head_block_ones(dtype):
    r = lax.broadcasted_iota(jnp.int32, (BRANCH_W, BRANCH_W), 0) // HEAD_W
    c = lax.broadcasted_iota(jnp.int32, (BRANCH_W, BRANCH_W), 1) // HEAD_W
    return (r == c).astype(dtype)


def _head_stack_mask(m, dtype):
    r = lax.broadcasted_iota(jnp.int32, (N_HEADS * m, BRANCH_W), 0) // m
    c = lax.broadcasted_iota(jnp.int32, (N_HEADS * m, BRANCH_W), 1) // HEAD_W
    return (r == c).astype(dtype)


def _head_norm(o, gain, ones_bd):
    msq = _dot_split(o * o, ones_bd, terms=2) * (1.0 / HEAD_W)
    return o * lax.rsqrt(msq + NORM_EPS) * gain


def _chunk_lower_tri(n, chunk, dtype):
    r = lax.broadcasted_iota(jnp.int32, (n, n), 0)
    c = lax.broadcasted_iota(jnp.int32, (n, n), 1)
    return ((c <= r) & (r // chunk == c // chunk)).astype(dtype)


class _LayerParam(NamedTuple):
    array: jax.Array
    layer: int


def _const_spec(shape):
    zeros = (0,) * len(shape)
    return pl.BlockSpec(shape, lambda *_: zeros, pipeline_mode=pl.Buffered(1))


def _param_specs(consts):
    specs, operands = [], []
    for c in consts:
        if isinstance(c, _LayerParam):
            tail = c.array.shape[1:]
            idx = (c.layer,) + (0,) * len(tail)
            specs.append(pl.BlockSpec((None,) + tail, lambda *_, idx=idx: idx, pipeline_mode=pl.Buffered(1)))
            operands.append(c.array)
        else:
            specs.append(_const_spec(c.shape))
            operands.append(c)
    return specs, operands


def _rope_kernel(pos_ref, freq_ref, cos_out, sin_out):
    ang = pos_ref[...].astype(F32) * freq_ref[...]
    cos_out[...] = jnp.cos(ang)
    sin_out[...] = jnp.sin(ang)


def _rope_tables(pos2, freq, tm):
    n = pos2.shape[0]
    row = lambda w: pl.BlockSpec((tm, w), lambda i: (i, 0))
    out = jax.ShapeDtypeStruct((n, LANES), F32)
    return pl.pallas_call(
        _rope_kernel,
        grid=(n // tm,),
        in_specs=[row(1), _const_spec(freq.shape)],
        out_specs=[row(LANES), row(LANES)],
        out_shape=[out, out],
        compiler_params=pltpu.CompilerParams(dimension_semantics=("parallel",), vmem_limit_bytes=VMEM_LIMIT),
        name="rope",
    )(pos2, freq)


def _front_kernel(x_ref, cos_ref, sin_ref, g_ref, wmix_ref, qn_ref, kvn_ref, wq_ref, wqr_ref, wk_ref, wvt_ref,
                  vones_ref, q_out, k_out, v_out, pb_out, pc_out, pd_lo_out, pd_hi_out):
    hn = _rms(x_ref[...], g_ref[...]).astype(BF16)
    pa = _dot(hn, wmix_ref[:, 0:W_SEC_A])
    pb_out[...] = _dot(hn, wmix_ref[:, W_SEC_A:W_SEC_A + W_SEC_B])
    pc_out[:, 0:W_SEC_C_MAIN] = _dot(hn, wmix_ref[:, W_SEC_A + W_SEC_B:W_SEC_A + W_SEC_B + W_SEC_C_MAIN])
    pc_out[:, W_SEC_C_MAIN:W_SEC_C] = pa[:, OFF_MLGATE:OFF_MLGATE + LANES]
    pd = _dot(hn, wmix_ref[:, W_SEC_A + W_SEC_B + W_SEC_C_MAIN:W_MIX])
    pd_lo_out[...] = pd[:, :LANES]
    pd_hi_out[...] = pd[:, LANES:]

    cos, sin = cos_ref[...], sin_ref[...]
    scale = (MLA_NOPE + MLA_ROPE) ** -0.5

    aqn = _rms(pa[:, OFF_AQ:OFF_AQ + MLA_Q_RANK], qn_ref[...]).astype(BF16)
    q0 = _dot(aqn, wq_ref[...])
    qr = _dot(aqn, wqr_ref[...])
    akvn = _rms(pa[:, OFF_AKV:OFF_AKV + MLA_KV_RANK], kvn_ref[...]).astype(BF16)
    kn = _dot(akvn, wk_ref[...])
    vt = _nt_dot(wvt_ref[...], akvn) + vones_ref[...]
    for t in range(v_out.shape[0]):
        v_out[t] = vt[:, t * ATT_BLK:(t + 1) * ATT_BLK].astype(v_out.dtype)
    k_rope = pa[:, OFF_KR:OFF_KR + LANES] * cos + pa[:, OFF_KRROT:OFF_KRROT + LANES] * sin
    for h in range(N_HEADS):
        sl = slice(h * LANES, (h + 1) * LANES)
        q_out[:, sl] = ((q0[:, sl] * cos + qr[:, sl] * sin) * scale).astype(q_out.dtype)
        k_out[:, sl] = (kn[:, sl] + k_rope).astype(k_out.dtype)


def _front(x2, rope_cos, rope_sin, lw, tm):
    n = x2.shape[0]
    row = lambda w: pl.BlockSpec((tm, w), lambda i: (i, 0))
    specs, consts = _param_specs([lw['g_mix_pre'], lw['w_mix'], lw['mla_q_norm'], lw['mla_kv_norm'], lw['wq'],
                                  lw['wqr'], lw['wk'], lw['wv_t'], lw['v_ones']])
    return pl.pallas_call(
        _front_kernel,
        grid=(n // tm,),
        in_specs=[row(D_MODEL), row(LANES), row(LANES)] + specs,
        out_specs=[row(4 * LANES), row(4 * LANES),
                   pl.BlockSpec((tm // ATT_BLK, 4 * LANES, ATT_BLK), lambda i: (i, 0, 0)), row(W_SEC_B),
                   row(W_SEC_C), row(LANES), row(LANES)],
        out_shape=[jax.ShapeDtypeStruct((n, 4 * LANES), BF16), jax.ShapeDtypeStruct((n, 4 * LANES), BF16),
                   jax.ShapeDtypeStruct((n // ATT_BLK, 4 * LANES, ATT_BLK), BF16),
                   jax.ShapeDtypeStruct((n, W_SEC_B), F32),
                   jax.ShapeDtypeStruct((n, W_SEC_C), F32), jax.ShapeDtypeStruct((n, LANES), F32),
                   jax.ShapeDtypeStruct((n, LANES), F32)],
        compiler_params=pltpu.CompilerParams(dimension_semantics=("parallel",), vmem_limit_bytes=VMEM_LIMIT),
        name="front",
    )(x2, rope_cos, rope_sin, *consts)


def _mla_kernel(q_ref, k_ref, vt_ref, o_ref, *, blk):
    i = pl.program_id(1)
    key = lax.broadcasted_iota(jnp.int32, (blk, blk), 0)
    qry = lax.broadcasted_iota(jnp.int32, (blk, blk), 1)
    causal = key <= qry

    chains = [(s, slice(h * LANES, (h + 1) * LANES)) for s in range(q_ref.shape[0]) for h in range(N_HEADS)]

    def step(j, carry, masked):
        start = pl.multiple_of(j * blk, blk)
        scores = [_nt_dot(k_ref[s, pl.ds(start, blk), sl], q_ref[s, :, sl]) for s, sl in chains]
        if masked:
            scores = [jnp.where(causal, sc, MASK_VALUE) for sc in scores]
        m_new = [jnp.maximum(c[0], jnp.max(sc, axis=0, keepdims=True)) for c, sc in zip(carry, scores)]
        probs = [jnp.exp(sc - m).astype(BF16) for sc, m in zip(scores, m_new)]
        pv = [_dot(vt_ref[s, j, sl, :], p) for (s, sl), p in zip(chains, probs)]
        return tuple((m, jnp.exp(c[0] - m) * c[1] + x) for c, m, x in zip(carry, m_new, pv))

    init = tuple((jnp.full((1, blk), MASK_VALUE, F32), jnp.zeros((LANES, blk), F32)) for _ in chains)
    carry = lax.fori_loop(0, i, functools.partial(step, masked=False), init)
    carry = step(i, carry, True)
    outs = [acc[:HEAD_W] * (1.0 / acc[HEAD_W:HEAD_W + 1]) for _, acc in carry]
    for s in range(q_ref.shape[0]):
        o_ref[s] = jnp.concatenate(outs[s * N_HEADS:(s + 1) * N_HEADS], axis=0).T.astype(o_ref.dtype)


def _mla(q, k, vt, batch, seq):
    blk = ATT_BLK
    q3, k3 = (a.reshape(batch, seq, 4 * LANES) for a in (q, k))
    vt4 = vt.reshape(batch, seq // blk, 4 * LANES, blk)
    nseq = ATT_SEQS if batch % ATT_SEQS == 0 else 1
    out = pl.pallas_call(
        functools.partial(_mla_kernel, blk=blk),
        grid=(batch // nseq, seq // blk),
        in_specs=[pl.BlockSpec((nseq, blk, 4 * LANES), lambda b, i: (b, i, 0)),
                  pl.BlockSpec((nseq, seq, 4 * LANES), lambda b, i: (b, 0, 0)),
                  pl.BlockSpec((nseq, seq // blk, 4 * LANES, blk), lambda b, i: (b, 0, 0, 0))],
        out_specs=pl.BlockSpec((nseq, blk, BRANCH_W), lambda b, i: (b, i, 0)),
        out_shape=jax.ShapeDtypeStruct((batch, seq, BRANCH_W), BF16),
        compiler_params=pltpu.CompilerParams(dimension_semantics=("parallel", "arbitrary"),
                                             vmem_limit_bytes=VMEM_LIMIT),
        name="mla",
    )(q3, k3, vt4)
    return out.reshape(batch * seq, BRANCH_W)


def _hgrn_kernel(pb_ref, lbl_ref, gain_ref, o_ref, st_ref, q_sc, k_sc, b_sc, lk_sc, o_sc, *, layer, tb):
    @pl.when(pl.program_id(1) == 0)
    def _():
        st_ref[...] = jnp.zeros_like(st_ref)

    lg = lbl_ref[...]
    e = jnp.exp(lg - jnp.max(lg, axis=0, keepdims=True))
    sm = e / jnp.sum(e, axis=0, keepdims=True)
    lb = jnp.zeros((1, BRANCH_W), F32)
    for r in range(1, layer + 1):
        lb = lb + sm[r:r + 1, :]
    one_m_lb = 1.0 - lb

    ones_bd = _head_block_ones(BF16)
    ones_bd_f32 = _head_block_ones(F32)
    row_in_sub = lax.broadcasted_iota(jnp.int32, (HG_CHUNK, BRANCH_W), 0) % SUB

    f_logit = pb_ref[:, BRANCH_W:2 * BRANCH_W]
    e = jnp.exp(-jnp.abs(f_logit))
    r = 1.0 / (1.0 + e)
    nonneg = f_logit >= 0.0
    forget = lb + one_m_lb * jnp.where(nonneg, r, e * r)
    b_all = _cumsum_rows(jnp.log(jnp.maximum(forget, GATE_FLOOR)), _chunk_lower_tri(tb, HG_CHUNK, BF16))
    b_sc[...] = b_all
    k_sc[...] = one_m_lb * jnp.where(nonneg, e * r, r)
    lk_sc[...] = jnp.log(one_m_lb) - jnp.maximum(f_logit, 0.0) + jnp.log(r) - b_all
    q_sc[...] = _silu(pb_ref[:, 0:BRANCH_W])

    def chunk_body(c, carry):
        start = pl.multiple_of(c * HG_CHUNK, HG_CHUNK)
        rows = pl.ds(start, HG_CHUNK)
        q, k, b = q_sc[rows, :], k_sc[rows, :], b_sc[rows, :]
        v = pb_ref[rows, 2 * BRANCH_W:3 * BRANCH_W]
        b_end = b[HG_CHUNK - 1:HG_CHUNK, :]

        st = st_ref[...]
        o = _nt_dot((q * jnp.exp(b)).astype(BF16), st.astype(BF16))
        k_end = k * jnp.exp(b_end - b)
        st_ref[...] = (st * jnp.exp(b_end) + _tn_dot(v.astype(BF16), k_end.astype(BF16))) * ones_bd_f32

        m = SUB
        while m < HG_CHUNK:
            mask = _head_stack_mask(m, F32)
            parts = []
            for lo in range(0, HG_CHUNK, 2 * m):
                left = slice(lo, lo + m)
                right = slice(lo + m, lo + 2 * m)
                ref = b[lo + m - 1:lo + m, :]
                kt = k[left] * jnp.exp(ref - b[left])
                qt = q[right] * jnp.exp(b[right] - ref)
                k_hat = (jnp.concatenate([kt] * N_HEADS, axis=0) * mask).astype(BF16)
                v_hat = (jnp.concatenate([v[left]] * N_HEADS, axis=0) * mask).astype(BF16)
                a = _nt_dot(qt.astype(BF16), k_hat)
                parts.append(jnp.zeros((m, BRANCH_W), F32))
                parts.append(_dot(a.astype(BF16), v_hat))
            o = o + jnp.concatenate(parts, axis=0)
            m *= 2

        def key_row(ref, col, s):
            return jnp.concatenate(
                [jnp.broadcast_to(ref[pl.ds(start + blk * SUB + s, 1), col:col + BRANCH_W], (SUB, BRANCH_W))
                 for blk in range(HG_CHUNK // SUB)], axis=0)

        for s in range(SUB):
            expo = b + key_row(lk_sc, 0, s)
            if s > 0:
                expo = jnp.where(row_in_sub >= s, expo, MASK_VALUE)
            score = _dot((q * jnp.exp(expo)).astype(BF16), ones_bd)
            o = o + score * key_row(pb_ref, 2 * BRANCH_W, s)
        o_sc[rows, :] = o
        return carry

    lax.fori_loop(0, tb // HG_CHUNK, chunk_body, 0, unroll=True)

    gate = _silu(pb_ref[:, 3 * BRANCH_W:4 * BRANCH_W])
    o_ref[...] = (_head_norm(o_sc[...], gain_ref[...], ones_bd) * gate).astype(o_ref.dtype)


def _hgrn(pb, lb_logits, gain, layer, batch, seq, tb):
    pb3 = pb.reshape(batch, seq, W_SEC_B)
    specs, consts = _param_specs([lb_logits, gain])
    blk_scratch = pltpu.VMEM((tb, BRANCH_W), F32)
    out = pl.pallas_call(
        functools.partial(_hgrn_kernel, layer=layer, tb=tb),
        grid=(batch, seq // tb),
        in_specs=[pl.BlockSpec((None, tb, W_SEC_B), lambda b, i: (b, i, 0))] + specs,
        out_specs=pl.BlockSpec((None, tb, BRANCH_W), lambda b, i: (b, i, 0)),
        out_shape=jax.ShapeDtypeStruct((batch, seq, BRANCH_W), BF16),
        scratch_shapes=[pltpu.VMEM((BRANCH_W, BRANCH_W), F32)] + [blk_scratch] * 5,
        compiler_params=pltpu.CompilerParams(dimension_semantics=("parallel", "arbitrary"),
                                             vmem_limit_bytes=VMEM_LIMIT),
        name="hgrn",
    )(pb3, *consts)
    return out.reshape(batch * seq, BRANCH_W)


def _mlstm_kernel(pc_ref, cw_ref, cb_ref, wq_ref, wk_ref, ib_ref, fb_ref, gain_ref, o_ref,
                  xext_ref, q_sc, k_sc, il_sc, b_sc, h_sc, c_ref, n_ref, m_ref, *, tb):
    pad = 8
    seqs = range(pc_ref.shape[0])

    @pl.when(pl.program_id(1) == 0)
    def _():
        xext_ref[:, 0:pad, :] = jnp.zeros((len(seqs), pad, BRANCH_W), F32)
        c_ref[...] = jnp.zeros_like(c_ref)
        n_ref[...] = jnp.zeros_like(n_ref)
        m_ref[...] = jnp.zeros_like(m_ref)

    gr = lax.broadcasted_iota(jnp.int32, (LANES, BRANCH_W), 0)
    gc = lax.broadcasted_iota(jnp.int32, (LANES, BRANCH_W), 1) // HEAD_W
    tri = _chunk_lower_tri(tb, CHUNK, BF16)
    for s in seqs:
        xext_ref[s, pad:pad + tb, :] = pc_ref[s, :, 0:BRANCH_W]
        conv = jnp.zeros((tb, BRANCH_W), F32) + cb_ref[...]
        for j in range(ML_CONV):
            first = pad - (ML_CONV - 1) + j
            conv = conv + xext_ref[s, first:first + tb, :] * cw_ref[j:j + 1, :]
        xext_ref[s, 0:pad, :] = xext_ref[s, tb:tb + pad, :]
        xc = _silu(conv).astype(BF16)
        q_sc[s] = _dot(xc, wq_ref[...])
        k_sc[s] = _dot(xc, wk_ref[...]) * (HEAD_W ** -0.5)
        gates = pc_ref[s, :, 3 * BRANCH_W:3 * BRANCH_W + LANES]
        il_sc[s] = _dot_split(gates, (gr == gc).astype(BF16)) + ib_ref[...]
        f_log = _log_sigmoid(_dot_split(gates, (gr == gc + N_HEADS).astype(BF16)) + fb_ref[...])
        b_sc[s] = _cumsum_rows(f_log, tri)

    ones_bd = _head_block_ones(BF16)
    ones_bd_f32 = _head_block_ones(F32)
    stack_mask = _head_stack_mask(CHUNK, F32)
    lane = lax.broadcasted_iota(jnp.int32, (CHUNK, BRANCH_W), 1)
    rowi = lax.broadcasted_iota(jnp.int32, (CHUNK, BRANCH_W), 0)
    diag_sel = (lane % HEAD_W == rowi).astype(F32)
    causal = (lane % HEAD_W) <= rowi

    def head_max(a):
        out = jnp.zeros_like(a)
        for h in range(N_HEADS):
            mx = jnp.max(a[:, h * HEAD_W:(h + 1) * HEAD_W], axis=-1, keepdims=True)
            out = jnp.where(lane // HEAD_W == h, mx, out)
        return out

    def chunk_body(c, carry):
        rows = pl.ds(pl.multiple_of(c * CHUNK, CHUNK), CHUNK)
        q = [q_sc[s, rows, :] for s in seqs]
        k = [k_sc[s, rows, :] for s in seqs]
        i_log = [il_sc[s, rows, :] for s in seqs]
        b = [b_sc[s, rows, :] for s in seqs]
        v = [pc_ref[s, rows, BRANCH_W:2 * BRANCH_W] for s in seqs]
        qb = [a.astype(BF16) for a in q]
        k_hat = [(jnp.concatenate([a] * N_HEADS, axis=0) * stack_mask).astype(BF16) for a in k]
        v_hat = [(jnp.concatenate([a] * N_HEADS, axis=0) * stack_mask).astype(BF16) for a in v]
        scores = [_nt_dot(qb[s], k_hat[s]) for s in seqs]
        c0, n0, m0 = [c_ref[s] for s in seqs], [n_ref[s] for s in seqs], [m_ref[s] for s in seqs]
        inter_num = [_dot(qb[s], c0[s].astype(BF16)) for s in seqs]
        inter_den = [_dot((q[s] * n0[s]).astype(BF16), ones_bd) for s in seqs]

        b_end = [a[CHUNK - 1:CHUNK, :] for a in b]
        g_end = [b_end[s] - b[s] + i_log[s] for s in seqs]
        m_new = [jnp.maximum(b_end[s] + m0[s], jnp.max(g_end[s], axis=0, keepdims=True)) for s in seqs]
        kw = [k[s] * jnp.exp(g_end[s] - m_new[s]) for s in seqs]
        kv = [_tn_dot(kw[s].astype(BF16), v[s].astype(BF16)) for s in seqs]

        b_row = [jnp.sum(a * diag_sel, axis=0, keepdims=True) for a in b]
        i_row = [jnp.sum(a * diag_sel, axis=0, keepdims=True) for a in i_log]
        dmat = [jnp.where(causal, b[s] - b_row[s] + i_row[s], MASK_VALUE) for s in seqs]
        a_t = [b[s] + m0[s] for s in seqs]
        m_t = [jnp.maximum(a_t[s], head_max(dmat[s])) for s in seqs]
        w_inter = [jnp.exp(a_t[s] - m_t[s]) for s in seqs]
        qk = [(scores[s] * jnp.exp(dmat[s] - m_t[s])).astype(BF16) for s in seqs]
        num = [_dot(qk[s], v_hat[s]) for s in seqs]
        den = [_dot(qk[s], ones_bd) for s in seqs]
        for s in seqs:
            h_num = num[s] + w_inter[s] * inter_num[s]
            h_den = den[s] + w_inter[s] * inter_den[s]
            h_sc[s, rows, :] = h_num / jnp.maximum(jnp.abs(h_den), jnp.exp(-m_t[s]))
            keep = jnp.exp(b_end[s] + m0[s] - m_new[s])
            c_ref[s] = (keep * c0[s] + kv[s]) * ones_bd_f32
            n_ref[s] = keep * n0[s] + jnp.sum(kw[s], axis=0, keepdims=True)
            m_ref[s] = m_new[s]
        return carry

    lax.fori_loop(0, tb // CHUNK, chunk_body, 0, unroll=2)

    for s in seqs:
        out_gate = _sigmoid(pc_ref[s, :, 2 * BRANCH_W:3 * BRANCH_W])
        o_ref[s] = (out_gate * _head_norm(h_sc[s], gain_ref[...], ones_bd)).astype(o_ref.dtype)


def _mlstm(pc, lw, batch, seq, tb):
    pc3 = pc.reshape(batch, seq, W_SEC_C)
    specs, consts = _param_specs([lw['ml_conv_w'], lw['ml_conv_b'], lw['ml_wq_bd'], lw['ml_wk_bd'],
                                  lw['ml_i_bias'], lw['ml_f_bias'], lw['ml_out_norm']])
    nseq = MIX_SEQS if batch % MIX_SEQS == 0 else 1
    blk_scratch = pltpu.VMEM((nseq, tb, BRANCH_W), F32)
    out = pl.pallas_call(
        functools.partial(_mlstm_kernel, tb=tb),
        grid=(batch // nseq, seq // tb),
        in_specs=[pl.BlockSpec((nseq, tb, W_SEC_C), lambda b, i: (b, i, 0))] + specs,
        out_specs=pl.BlockSpec((nseq, tb, BRANCH_W), lambda b, i: (b, i, 0)),
        out_shape=jax.ShapeDtypeStruct((batch, seq, BRANCH_W), BF16),
        scratch_shapes=[pltpu.VMEM((nseq, tb + 8, BRANCH_W), F32)] + [blk_scratch] * 5
                       + [pltpu.VMEM((nseq, BRANCH_W, BRANCH_W), F32), pltpu.VMEM((nseq, 1, BRANCH_W), F32),
                          pltpu.VMEM((nseq, 1, BRANCH_W), F32)],
        compiler_params=pltpu.CompilerParams(dimension_semantics=("parallel", "arbitrary"),
                                             vmem_limit_bytes=VMEM_LIMIT),
        name="mlstm",
    )(pc3, *consts)
    return out.reshape(batch * seq, BRANCH_W)


def _lane_block_transpose(rows):
    n = len(rows)
    width = rows[0].shape[1]
    blk = lax.broadcasted_iota(jnp.int32, rows[0].shape, 1) // (width // n)
    d = n // 2
    while d >= 1:
        upper = (blk & d) != 0
        shift = d * (width // n)
        nxt = list(rows)
        for i in range(n):
            if i & d == 0:
                lo, hi = rows[i], rows[i + d]
                nxt[i] = jnp.where(upper, pltpu.roll(hi, shift, 1), lo)
                nxt[i + d] = jnp.where(upper, hi, pltpu.roll(lo, width - shift, 1))
        rows = nxt
        d //= 2
    return rows


def _s5_kernel(u_lo_ref, u_hi_ref, tz_ref, min_re_ref, min_im_ref, mout_re_ref, mout_im_ref, pw_re_ref,
               pw_im_ref, d_ref, y_lo_ref, y_hi_ref, ug_ref, yg_ref, *, n_rows):
    nb = LANES // S5_GROUP_CH
    for tok_half in range(S5_CHUNK // nb):
        for grp_half, u_ref in enumerate((u_lo_ref, u_hi_ref)):
            by_token = [u_ref[pl.ds(nb * tok_half + t, n_rows, stride=S5_CHUNK), :] for t in range(nb)]
            by_group = _lane_block_transpose(by_token)
            for g in range(nb):
                ug_ref[nb * grp_half + g, :, tok_half * LANES:(tok_half + 1) * LANES] = by_group[g]

    row = lax.broadcasted_iota(jnp.int32, (n_rows, LANES), 0)
    for pair in range(S5_GROUPS // 2):
        u = [ug_ref[2 * pair + s] for s in range(2)]
        ub = [a.astype(BF16) for a in u]
        s_re = _dot(ub[0], min_re_ref[2 * pair]) + _dot(ub[1], min_re_ref[2 * pair + 1])
        s_im = _dot(ub[0], min_im_ref[2 * pair]) + _dot(ub[1], min_im_ref[2 * pair + 1])
        step, k = 1, 0
        while step < n_rows:
            keep = row >= step
            p_re = jnp.where(keep, pltpu.roll(s_re, step, 0), 0.0)
            p_im = jnp.where(keep, pltpu.roll(s_im, step, 0), 0.0)
            a_re = pw_re_ref[pair, k:k + 1, :]
            a_im = pw_im_ref[pair, k:k + 1, :]
            s_re, s_im = s_re + a_re * p_re - a_im * p_im, s_im + a_re * p_im + a_im * p_re
            step, k = step * 2, k + 1
        first = row >= 1
        s0_re = jnp.where(first, pltpu.roll(s_re, 1, 0), 0.0).astype(BF16)
        s0_im = jnp.where(first, pltpu.roll(s_im, 1, 0), 0.0).astype(BF16)
        for s in range(2):
            g = 2 * pair + s
            yg_ref[g] = (_dot(ub[s], tz_ref[g]) + _dot(s0_re, mout_re_ref[g]) + _dot(s0_im, mout_im_ref[g])
                         + u[s] * d_ref[g])

    for tok_half in range(S5_CHUNK // nb):
        for grp_half, y_ref in enumerate((y_lo_ref, y_hi_ref)):
            by_group = [yg_ref[nb * grp_half + g, :, tok_half * LANES:(tok_half + 1) * LANES] for g in range(nb)]
            by_token = _lane_block_transpose(by_group)
            for t in range(nb):
                y_ref[pl.ds(nb * tok_half + t, n_rows, stride=S5_CHUNK), :] = by_token[t]


def _s5(pd_lo, pd_hi, lw, batch, seq):
    n_rows = seq // S5_CHUNK
    assert n_rows <= 2 ** S5_SCAN_STEPS
    width = S5_CHUNK * S5_GROUP_CH
    specs, consts = _param_specs([lw['s5_tz'], lw['s5_min_re'], lw['s5_min_im'], lw['s5_mout_re'],
                                  lw['s5_mout_im'], lw['s5_pw_re'], lw['s5_pw_im'], lw['s5_d']])
    seq_blk = pl.BlockSpec((None, seq, LANES), lambda b: (b, 0, 0))
    out = jax.ShapeDtypeStruct((batch, seq, LANES), F32)
    y_lo, y_hi = pl.pallas_call(
        functools.partial(_s5_kernel, n_rows=n_rows),
        grid=(batch,),
        in_specs=[seq_blk, seq_blk] + specs,
        out_specs=[seq_blk, seq_blk],
        out_shape=[out, out],
        scratch_shapes=[pltpu.VMEM((S5_GROUPS, n_rows, width), F32), pltpu.VMEM((S5_CHUNK, n_rows, BRANCH_W), F32)],
        compiler_params=pltpu.CompilerParams(dimension_semantics=("parallel",), vmem_limit_bytes=VMEM_LIMIT),
        name="s5",
    )(pd_lo.reshape(batch, seq, LANES), pd_hi.reshape(batch, seq, LANES), *consts)
    return y_lo.reshape(batch * seq, LANES), y_hi.reshape(batch * seq, LANES)


def _merge_kernel(x_ref, a_ref, b_ref, c_ref, y_lo_ref, y_hi_ref, gpre_ref, wgate_ref, wbr_ref, wout_ref,
                  wglu_ref, bglu_ref, gpost_ref, o_ref):
    x = x_ref[...]
    hn = _rms(x, gpre_ref[...]).astype(BF16)
    y = _gelu_tanh(jnp.concatenate([y_lo_ref[...], y_hi_ref[...]], axis=-1))
    out_d = (y * _sigmoid(_dot(y.astype(BF16), wglu_ref[...]) + bglu_ref[...])).astype(BF16)
    branches = (a_ref[...], b_ref[...], c_ref[...], out_d)
    merged = None
    for n in range(4):
        gate = _sigmoid(_dot(hn, wgate_ref[:, n * D_MODEL:(n + 1) * D_MODEL]))
        term = gate * _dot(branches[n], wbr_ref[n])
        merged = term if merged is None else merged + term
    mix = _dot(merged.astype(BF16), wout_ref[...])
    o_ref[...] = x + _rms(mix, gpost_ref[...])


def _merge(x2, out_a, out_b, out_c, y_lo, y_hi, lw, tm):
    n = x2.shape[0]
    row = lambda w: pl.BlockSpec((tm, w), lambda i: (i, 0))
    specs, consts = _param_specs([lw['g_mix_pre'], lw['w_gate'], lw['w_branch'], lw['w_out'], lw['s5_w_glu'],
                                  lw['s5_b_glu'], lw['g_mix_post']])
    return pl.pallas_call(
        _merge_kernel,
        grid=(n // tm,),
        in_specs=[row(D_MODEL), row(BRANCH_W), row(BRANCH_W), row(BRANCH_W), row(LANES), row(LANES)] + specs,
        out_specs=row(D_MODEL),
        out_shape=jax.ShapeDtypeStruct((n, D_MODEL), F32),
        compiler_params=pltpu.CompilerParams(dimension_semantics=("parallel",), vmem_limit_bytes=VMEM_LIMIT),
        name="merge",
    )(x2, out_a, out_b, out_c, y_lo, y_hi, *consts)


def _memkv_kernel(mem_ref, g_ref, wk_ref, wv_ref, k_out, v_out):
    mn = _rms(mem_ref[...], g_ref[...]).astype(BF16)
    k_out[...] = _dot(mn, wk_ref[...].astype(BF16)).astype(k_out.dtype)
    v_out[...] = _dot(mn, wv_ref[...].astype(BF16)).astype(v_out.dtype)


def _memkv(mem, g_mem, wk, wv):
    batch, mlen, _ = mem.shape
    depth = wk.shape[0]
    per_layer = lambda a: pl.BlockSpec((None,) + a.shape[1:], lambda l, b: (l,) + (0,) * (a.ndim - 1))
    out_blk = pl.BlockSpec((None, None, mlen, D_MODEL), lambda l, b: (l, b, 0, 0))
    return pl.pallas_call(
        _memkv_kernel,
        grid=(depth, batch),
        in_specs=[pl.BlockSpec((None, mlen, D_MODEL), lambda l, b: (b, 0, 0)),
                  per_layer(g_mem), per_layer(wk), per_layer(wv)],
        out_specs=[out_blk, out_blk],
        out_shape=[jax.ShapeDtypeStruct((depth, batch, mlen, D_MODEL), BF16)] * 2,
        compiler_params=pltpu.CompilerParams(dimension_semantics=("parallel", "parallel"),
                                             vmem_limit_bytes=VMEM_LIMIT),
        name="memkv",
    )(mem, g_mem, wk, wv)


def _xattn_kernel(x_ref, k_ref, v_ref, gpre_ref, wq_ref, wo_ref, gpost_ref, o_ref):
    x = x_ref[...]
    hn = _rms(x, gpre_ref[...]).astype(BF16)
    q = (_dot(hn, wq_ref[...].astype(BF16)) * (XA_DIM ** -0.5)).astype(BF16)
    outs = []
    for h in range(XA_HEADS):
        sl = slice(h * XA_DIM, (h + 1) * XA_DIM)
        s = _nt_dot(q[:, sl], k_ref[:, sl])
        p = jnp.exp(s - jnp.max(s, axis=-1, keepdims=True))
        inv = 1.0 / jnp.sum(p, axis=-1, keepdims=True)
        outs.append((_dot(p.astype(BF16), v_ref[:, sl]) * inv).astype(BF16))
    xa = _dot(jnp.concatenate(outs, axis=-1), wo_ref[...].astype(BF16))
    o_ref[...] = x + _rms(xa, gpost_ref[...])


def _xattn(x2, mem_k, mem_v, layer, lw, batch, seq, tm):
    mlen = mem_k.shape[2]
    x3 = x2.reshape(batch, seq, D_MODEL)
    specs, consts = _param_specs([lw['g_xa_pre'], lw['xa_wq'], lw['xa_wo'], lw['g_xa_post']])
    row = pl.BlockSpec((None, tm, D_MODEL), lambda b, i: (b, i, 0))
    kv = pl.BlockSpec((None, None, mlen, D_MODEL), lambda b, i: (layer, b, 0, 0))
    out = pl.pallas_call(
        _xattn_kernel,
        grid=(batch, seq // tm),
        in_specs=[row, kv, kv] + specs,
        out_specs=row,
        out_shape=jax.ShapeDtypeStruct((batch, seq, D_MODEL), F32),
        compiler_params=pltpu.CompilerParams(dimension_semantics=("parallel", "parallel"),
                                             vmem_limit_bytes=VMEM_LIMIT),
        name="xattn",
    )(x3, mem_k, mem_v, *consts)
    return out.reshape(batch * seq, D_MODEL)


def _ffn_kernel(x_ref, gpre_ref, win_ref, wo_ref, gpost_ref, o_ref):
    x = x_ref[...]
    hn = _rms(x, gpre_ref[...]).astype(BF16)
    acc = None
    for j in range(D_FF // FF_CHUNK):
        lo = j * FF_CHUNK
        a = _dot(hn, win_ref[:, lo:lo + FF_CHUNK])
        b = _dot(hn, win_ref[:, D_FF + lo:D_FF + lo + FF_CHUNK])
        part = _dot((_silu(a) * b).astype(BF16), wo_ref[lo:lo + FF_CHUNK, :])
        acc = part if acc is None else acc + part
    o_ref[...] = x + _rms(acc, gpost_ref[...])


def _ffn(x2, lw, tm):
    n = x2.shape[0]
    row = pl.BlockSpec((tm, D_MODEL), lambda i: (i, 0))
    specs, consts = _param_specs([lw['g_ffn_pre'], lw['ffn_w_in'], lw['ffn_wo'], lw['g_ffn_post']])
    return pl.pallas_call(
        _ffn_kernel,
        grid=(n // tm,),
        in_specs=[row] + specs,
        out_specs=row,
        out_shape=jax.ShapeDtypeStruct((n, D_MODEL), F32),
        compiler_params=pltpu.CompilerParams(dimension_semantics=("parallel",), vmem_limit_bytes=VMEM_LIMIT),
        name="ffn",
    )(x2, *consts)


def _rot_half(w):
    half = MLA_ROPE // 2
    return jnp.concatenate([-w[..., half:], w[..., :half]], axis=-1)


def _block_diag(w):
    depth, h, d, e = w.shape
    eye = jnp.eye(h, dtype=w.dtype)
    return (eye[None, :, None, :, None] * w[:, :, :, None, :]).reshape(depth, h * d, h * e)


def _s5_tables(a_re, a_im, log_dt, b_re, b_im, c_re, c_im, d):
    hi = lax.Precision.HIGHEST
    g, p, hch = b_re.shape
    L = S5_CHUNK
    width = L * hch
    dt = jnp.exp(log_dt)[:, None]
    lam_re, lam_im = a_re * dt, a_im * dt
    mag = jnp.exp(lam_re)
    ab_re, ab_im = mag * jnp.cos(lam_im), mag * jnp.sin(lam_im)
    inv_abs2 = 1.0 / (a_re * a_re + a_im * a_im)
    z_re = ((ab_re - 1.0) * a_re + ab_im * a_im) * inv_abs2
    z_im = (ab_im * a_re - (ab_re - 1.0) * a_im) * inv_abs2
    bt_re, bt_im = b_re.swapaxes(1, 2), b_im.swapaxes(1, 2)
    bbt_re = z_re[:, None, :] * bt_re - z_im[:, None, :] * bt_im
    bbt_im = z_re[:, None, :] * bt_im + z_im[:, None, :] * bt_re

    def power(k):
        k = jnp.asarray(k, F32)[None, :, None]
        m = jnp.exp(k * lam_re[:, None, :])
        return m * jnp.cos(k * lam_im[:, None, :]), m * jnp.sin(k * lam_im[:, None, :])

    pw_re, pw_im = power(jnp.arange(L + 1))
    ct_re, ct_im = c_re.swapaxes(1, 2)[:, :, None, :], c_im.swapaxes(1, 2)[:, :, None, :]
    pk_re, pk_im = pw_re.swapaxes(1, 2)[:, :, :, None], pw_im.swapaxes(1, 2)[:, :, :, None]
    cp_re = (ct_re * pk_re - ct_im * pk_im).reshape(g, p, (L + 1) * hch)
    cp_im = (ct_re * pk_im + ct_im * pk_re).reshape(g, p, (L + 1) * hch)
    kern = (jnp.einsum('ghp,gpx->ghx', bbt_re, cp_re, precision=hi)
            - jnp.einsum('ghp,gpx->ghx', bbt_im, cp_im, precision=hi))
    x = jnp.arange((L + 1) * hch)
    y = jnp.arange(width)
    lag = y[None, None, :] // hch - jnp.arange(L)[:, None, None]
    place = ((x[None, :, None] // hch == lag) & (x[None, :, None] % hch == y[None, None, :] % hch)).astype(BF16)
    tz = jnp.einsum('ghx,lxy->glhy', kern.astype(BF16), place, preferred_element_type=F32)
    tz = tz.astype(BF16).reshape(g, width, width)

    rp_re, rp_im = pw_re[:, L - 1 - jnp.arange(L), None, :], pw_im[:, L - 1 - jnp.arange(L), None, :]
    min_re = (rp_re * bbt_re[:, None] - rp_im * bbt_im[:, None]).reshape(g, width, p)
    min_im = (rp_re * bbt_im[:, None] + rp_im * bbt_re[:, None]).reshape(g, width, p)
    mout_re, mout_im = cp_re[:, :, hch:], -cp_im[:, :, hch:]

    even_group = (jnp.arange(g) % 2 == 0)[:, None, None]

    def side_by_side(a, axis):
        z = jnp.zeros_like(a)
        return jnp.where(even_group, jnp.concatenate([a, z], axis=axis), jnp.concatenate([z, a], axis=axis))

    st_re, st_im = power(L * (2 ** jnp.arange(S5_SCAN_STEPS)))
    pair_lanes = lambda a: a.reshape(g // 2, 2, S5_SCAN_STEPS, p).swapaxes(1, 2).reshape(g // 2, S5_SCAN_STEPS, 2 * p)
    return dict(
        s5_tz=tz,
        s5_min_re=side_by_side(min_re, 2).astype(BF16), s5_min_im=side_by_side(min_im, 2).astype(BF16),
        s5_mout_re=side_by_side(mout_re, 1).astype(BF16), s5_mout_im=side_by_side(mout_im, 1).astype(BF16),
        s5_pw_re=pair_lanes(st_re), s5_pw_im=pair_lanes(st_im),
        s5_d=jnp.tile(d, (1, L)).reshape(g, 1, width),
    )


def _stacked_weights(p):
    depth = p['w_in'].shape[0]
    cols = lambda lo, hi: p['w_in'][:, :, lo:hi].astype(BF16)
    zeros = lambda n: jnp.zeros((depth, D_MODEL, n), BF16)
    kr = cols(IN_KR, IN_B)
    rope_pad = LANES - MLA_NOPE - MLA_ROPE
    w_mix = jnp.concatenate([
        cols(0, IN_KR),
        zeros(MLA_NOPE), kr, zeros(rope_pad),
        zeros(MLA_NOPE), _rot_half(kr), zeros(rope_pad),
        cols(IN_CG, IN_D), zeros(LANES - 2 * N_HEADS),
        cols(IN_B, IN_C),
        cols(IN_C, IN_CG),
        cols(IN_D, IN_GATE)], axis=2)
    assert w_mix.shape[2] == W_MIX

    uq = p['mla_w_uq'].astype(BF16).reshape(depth, MLA_Q_RANK, N_HEADS, MLA_NOPE + MLA_ROPE)
    zq = jnp.zeros((depth, MLA_Q_RANK, N_HEADS, rope_pad), BF16)
    wq = jnp.concatenate([uq, zq], axis=-1).reshape(depth, MLA_Q_RANK, N_HEADS * LANES)
    wqr = jnp.concatenate([jnp.zeros((depth, MLA_Q_RANK, N_HEADS, MLA_NOPE), BF16),
                           _rot_half(uq[..., MLA_NOPE:]), zq], axis=-1).reshape(depth, MLA_Q_RANK, N_HEADS * LANES)
    ukv = p['mla_w_ukv'].astype(BF16).reshape(depth, MLA_KV_RANK, N_HEADS, MLA_NOPE + HEAD_W)
    zk = jnp.zeros((depth, MLA_KV_RANK, N_HEADS, LANES - MLA_NOPE), BF16)
    wk = jnp.concatenate([ukv[..., :MLA_NOPE], zk], axis=-1).reshape(depth, MLA_KV_RANK, N_HEADS * LANES)
    wv_t = jnp.concatenate([ukv[..., MLA_NOPE:], zk], axis=-1).reshape(depth, MLA_KV_RANK, N_HEADS * LANES)
    wv_t = wv_t.swapaxes(1, 2)

    spread = lambda b: jnp.repeat(b, HEAD_W, axis=-1)[:, None, :]
    row = lambda a: a[:, None, :]
    sw = dict(
        g_mix_pre=row(p['norm_mix_pre']), g_mix_post=row(p['norm_mix_post']),
        w_mix=w_mix, w_gate=cols(IN_GATE, p['w_in'].shape[2]),
        mla_q_norm=row(p['mla_q_norm']), mla_kv_norm=row(p['mla_kv_norm']),
        wq=wq, wqr=wqr, wk=wk, wv_t=wv_t,
        hg_out_norm=row(p['hg_out_norm']),
        ml_conv_w=p['ml_conv_w'], ml_conv_b=row(p['ml_conv_b']),
        ml_wq_bd=_block_diag(p['ml_w_q'].astype(BF16)), ml_wk_bd=_block_diag(p['ml_w_k'].astype(BF16)),
        ml_i_bias=spread(p['ml_i_bias']), ml_f_bias=spread(p['ml_f_bias']),
        ml_out_norm=row(p['ml_out_norm']),
        s5_w_glu=p['s5_w_glu'].astype(BF16), s5_b_glu=row(p['s5_b_glu']),
        w_branch=p['w_branch'].astype(BF16), w_out=p['w_out'].astype(BF16),
        g_xa_pre=row(p['norm_xa_pre']), g_xa_post=row(p['norm_xa_post']), g_mem=row(p['norm_mem']),
        xa_wq=p['xa_wq'], xa_wk=p['xa_wk'], xa_wv=p['xa_wv'], xa_wo=p['xa_wo'],
        g_ffn_pre=row(p['norm_ffn_pre']), g_ffn_post=row(p['norm_ffn_post']),
        ffn_w_in=p['ffn_w_in'].astype(BF16), ffn_wo=p['ffn_w_out'].astype(BF16),
    )
    sw.update(jax.vmap(_s5_tables)(p['s5_a_re'], p['s5_a_im'], p['s5_log_dt'], p['s5_b_re'], p['s5_b_im'],
                                   p['s5_c_re'], p['s5_c_im'], p['s5_d']))
    return sw


def _shared_constants():
    half = MLA_ROPE // 2
    inv_freq = ROPE_THETA ** (-np.arange(half, dtype=np.float32) / half)
    freq = np.zeros((1, LANES), np.float32)
    freq[0, MLA_NOPE:MLA_NOPE + half] = inv_freq
    freq[0, MLA_NOPE + half:MLA_NOPE + MLA_ROPE] = inv_freq
    v_ones = np.zeros((N_HEADS * LANES, 1), np.float32)
    v_ones[HEAD_W::LANES] = 1.0
    return dict(freq=jnp.asarray(freq), v_ones=jnp.asarray(v_ones))


def kernel(x, mem, positions, norm_mix_pre, norm_mix_post, w_in, mla_q_norm, mla_w_uq, mla_kv_norm, mla_w_ukv, hg_lb_logits, hg_out_norm, ml_conv_w, ml_conv_b, ml_w_q, ml_w_k, ml_i_bias, ml_f_bias, ml_out_norm, s5_a_re, s5_a_im, s5_log_dt, s5_b_re, s5_b_im, s5_c_re, s5_c_im, s5_d, s5_w_glu, s5_b_glu, w_branch, w_out, norm_xa_pre, norm_xa_post, norm_mem, xa_wq, xa_wk, xa_wv, xa_wo, norm_ffn_pre, norm_ffn_post, ffn_w_in, ffn_w_out):
    p = dict(locals())
    batch, seq, _ = x.shape
    depth = w_in.shape[0]
    n = batch * seq
    tm = min(ROW_TILE, seq)
    tb = min(SEQ_TILE, seq)
    x2 = x.reshape(n, D_MODEL)
    pos2 = positions.reshape(n, 1).astype(jnp.int32)
    stacked = _stacked_weights(p)
    shared = _shared_constants()
    rope_cos, rope_sin = _rope_tables(pos2, shared['freq'], tm)
    mem_k, mem_v = _memkv(mem, stacked['g_mem'], stacked['xa_wk'], stacked['xa_wv'])
    for l in range(depth):
        lw = {name: _LayerParam(a, l) for name, a in stacked.items()}
        lw.update(shared)
        q, k, vt, pb, pc, pd_lo, pd_hi = _front(x2, rope_cos, rope_sin, lw, tm)
        out_a = _mla(q, k, vt, batch, seq)
        out_b = _hgrn(pb, hg_lb_logits, lw['hg_out_norm'], l, batch, seq, tb)
        out_c = _mlstm(pc, lw, batch, seq, tb)
        y_lo, y_hi = _s5(pd_lo, pd_hi, lw, batch, seq)
        x2 = _merge(x2, out_a, out_b, out_c, y_lo, y_hi, lw, tm)
        x2 = _xattn(x2, mem_k, mem_v, l, lw, batch, seq, tm)
        x2 = _ffn(x2, lw, tm)
    return x2.reshape(batch, seq, D_MODEL)
```
